```python
import math
import jax
import jax.numpy as jnp
from jax import lax
import numpy as np


D_MODEL = 2048
BATCH = 2
SEQ = 4096
DEPTH = 4
DEC_BATCH = 8
DEC_SEQ = 2048
PAST_LEN = 128

GRID_W = 64
Q_BLOCK = 128
NORM_EPS = 1e-6

SSD_HEADS = 16
SSD_HEAD_DIM = 64
SSD_WIDTH = SSD_HEADS * SSD_HEAD_DIM
SSD_GROUPS = 2
SSD_STATE = 128
SSD_CONV_W = 5
SSD_CHUNK = 128
SSD_CONV_DIM = SSD_WIDTH + 2 * SSD_GROUPS * SSD_STATE

GQA_HEADS = 4
GQA_KV_HEADS = 2
GQA_HEAD_DIM = 128
GQA_WIDTH = GQA_HEADS * GQA_HEAD_DIM
ROPE_THETA = 10000.0

DIFF_HEADS = 4
DIFF_QK_DIM = 64
DIFF_V_DIM = 128
DIFF_WIDTH = DIFF_HEADS * DIFF_V_DIM

MIX_WIDTH = SSD_WIDTH + GQA_WIDTH + DIFF_WIDTH
D_FF = -(-8 * D_MODEL // (3 * 256)) * 256

IN_SIZES = (
    SSD_WIDTH,
    SSD_CONV_DIM,
    2 * SSD_HEADS,
    GQA_WIDTH,
    GQA_KV_HEADS * GQA_HEAD_DIM,
    GQA_KV_HEADS * GQA_HEAD_DIM,
    DIFF_HEADS * 2 * DIFF_QK_DIM,
    DIFF_HEADS * 2 * DIFF_QK_DIM,
    DIFF_WIDTH,
)
IN_COLS = sum(IN_SIZES)

kernel_name = 'hybrid_bidir_ssd_gqa_diff_encoder'

F32 = jnp.float32


def rmsnorm(x, g):
    xf = x.astype(F32)
    y = xf * lax.rsqrt(jnp.mean(xf * xf, axis=-1, keepdims=True) + NORM_EPS)
    return (y * g.astype(F32)).astype(x.dtype)


def ada_modulation(c, w, b):
    m = jnp.einsum('bd,de->be', jax.nn.silu(c), w) + b
    return [t[:, None, :] for t in jnp.split(m, 6, axis=-1)]


def centred_depthwise_conv(x, w, b):
    k, ch = w.shape
    pad = k // 2
    y = lax.conv_general_dilated(
        x, w[:, None, :], window_strides=(1,), padding=[(pad, pad)],
        dimension_numbers=('NWC', 'WIO', 'NWC'), feature_group_count=ch)
    return y + b


def segsum_exp(a):
    cs = jnp.cumsum(a, axis=-1)
    diff = cs[..., :, None] - cs[..., None, :]
    n = a.shape[-1]
    mask = jnp.tril(jnp.ones((n, n), dtype=bool))
    return jnp.exp(jnp.where(mask, diff, -jnp.inf))


def ssd_scan(x, dt, a, bm, cm):
    b, s, h, p = x.shape
    g, n = bm.shape[-2:]
    r = h // g
    nc, l = s // SSD_CHUNK, SSD_CHUNK
    xc = (x * dt[..., None]).reshape(b, nc, l, g, r, p)
    ac = jnp.moveaxis((dt * a).reshape(b, nc, l, g, r), 2, -1)
    bc = bm.reshape(b, nc, l, g, n)
    cc = cm.reshape(b, nc, l, g, n)
    a_cs = jnp.cumsum(ac, axis=-1)
    decay = segsum_exp(ac)
    scores = jnp.einsum('bclgn,bcsgn->bcgls', cc, bc)
    y_diag = jnp.einsum('bcgrls,bcsgrp->bclgrp', scores[:, :, :, None] * decay, xc)
    decay_to_end = jnp.moveaxis(jnp.exp(a_cs[..., -1:] - a_cs), -1, 2)
    chunk_states = jnp.einsum('bcsgn,bcsgrp->bcgrpn', bc, xc * decay_to_end[..., None])
    chunk_decay = jnp.exp(a_cs[..., -1])

    def step(carry, inp):
        st, dec = inp
        return carry * dec[..., None, None] + st, carry

    init = jnp.zeros((b, g, r, p, n), dtype=chunk_states.dtype)
    _, prev = lax.scan(step, init, (jnp.moveaxis(chunk_states, 1, 0),
                                    jnp.moveaxis(chunk_decay, 1, 0)))
    prev = jnp.moveaxis(prev, 0, 1)
    decay_in = jnp.moveaxis(jnp.exp(a_cs), -1, 2)
    y_off = jnp.einsum('bclgn,bcgrpn->bclgrp', cc, prev) * decay_in[..., None]
    return (y_diag + y_off).reshape(b, s, h, p)


def ssd_mixer(z, xbc, dt_raw, conv_w, conv_b, dt_bias, a_log, d_skip, norm_g):
    b, s, _ = xbc.shape
    xbc = jax.nn.silu(centred_depthwise_conv(xbc, conv_w, conv_b))
    xs, bm, cm = jnp.split(xbc, [SSD_WIDTH, SSD_WIDTH + SSD_GROUPS * SSD_STATE], axis=-1)
    xs = xs.reshape(b, s, SSD_HEADS, SSD_HEAD_DIM).astype(F32)
    bm = bm.reshape(b, s, SSD_GROUPS, SSD_STATE).astype(F32)
    cm = cm.reshape(b, s, SSD_GROUPS, SSD_STATE).astype(F32)
    dt = jax.nn.softplus(dt_raw.astype(F32).reshape(b, s, 2, SSD_HEADS)
                         + dt_bias.astype(F32))
    a = -jnp.exp(a_log.astype(F32))
    y_fwd = ssd_scan(xs, dt[:, :, 0], a[0], bm, cm)
    flip = lambda t: jnp.flip(t, axis=1)
    y_bwd = flip(ssd_scan(flip(xs), flip(dt[:, :, 1]), a[1], flip(bm), flip(cm)))
    y = y_fwd + y_bwd + xs * d_skip.astype(F32)[:, None]
    y = y.reshape(b, s, SSD_WIDTH) * jax.nn.silu(z.astype(F32))
    yg = y.reshape(b, s, SSD_GROUPS, SSD_WIDTH // SSD_GROUPS)
    yg = yg * lax.rsqrt(jnp.mean(yg * yg, axis=-1, keepdims=True) + NORM_EPS)
    return (yg.reshape(b, s, SSD_WIDTH) * norm_g.astype(F32)).astype(z.dtype)


def axial_rope_angles(seq):
    rows = seq // GRID_W
    row_idx, col_idx = jnp.meshgrid(jnp.arange(rows), jnp.arange(GRID_W), indexing='ij')
    row_idx = row_idx.reshape(-1).astype(F32)
    col_idx = col_idx.reshape(-1).astype(F32)
    axis_dim = GQA_HEAD_DIM // 2
    inv_freq = ROPE_THETA ** (-jnp.arange(0, axis_dim, 2, dtype=F32) / axis_dim)
    return row_idx[:, None] * inv_freq, col_idx[:, None] * inv_freq


def rope_axis(x, ang):
    cos = jnp.cos(ang)[:, None, :].astype(x.dtype)
    sin = jnp.sin(ang)[:, None, :].astype(x.dtype)
    x1, x2 = jnp.split(x, 2, axis=-1)
    return jnp.concatenate([x1 * cos - x2 * sin, x2 * cos + x1 * sin], axis=-1)


def axial_rope(x, ang_r, ang_c):
    xr, xc = jnp.split(x, 2, axis=-1)
    return jnp.concatenate([rope_axis(xr, ang_r), rope_axis(xc, ang_c)], axis=-1)


def gqa_attention(q, k, v):
    b, s, hq, d = q.shape
    r = hq // GQA_KV_HEADS
    nb = s // Q_BLOCK
    qb = jnp.moveaxis(q.reshape(b, nb, Q_BLOCK, GQA_KV_HEADS, r, d), 1, 0)
    scale = d ** -0.5

    def block(qblk):
        sc = jnp.einsum('blgrd,bsgd->bgrls', qblk, k, preferred_element_type=F32) * scale
        p = jax.nn.softmax(sc, axis=-1)
        return jnp.einsum('bgrls,bsgd->blgrd', p.astype(v.dtype), v)

    o = lax.map(block, qb)
    return jnp.moveaxis(o, 0, 1).reshape(b, s, hq * d)


def diff_attention(q, k, v, lam, slopes):
    b, s, h, _, dk = q.shape
    dv = v.shape[-1]
    nb = s // Q_BLOCK
    qb = jnp.moveaxis(q.reshape(b, nb, Q_BLOCK, h, 2, dk), 1, 0)
    kpos = jnp.arange(s, dtype=F32)
    qpos = kpos.reshape(nb, Q_BLOCK)
    scale = dk ** -0.5

    def block(args):
        qblk, qp = args
        sc = jnp.einsum('blhmd,bshmd->bhmls', qblk, k, preferred_element_type=F32) * scale
        alibi = -slopes[:, None, None] * jnp.abs(qp[:, None] - kpos[None, :])
        p = jax.nn.softmax(sc + alibi[None, :, None], axis=-1)
        attn = p[:, :, 0] - lam * p[:, :, 1]
        return jnp.einsum('bhls,bshe->blhe', attn.astype(v.dtype), v)

    o = lax.map(block, (qb, qpos))
    return jnp.moveaxis(o, 0, 1).reshape(b, s, h, dv)


def encoder_layer(x, c, ang_r, ang_c, li, w_mod, b_mod, norm1_g, w_in, conv_w, conv_b,
                  dt_bias, a_log, d_skip, ssd_norm_g, q_norm_g, k_norm_g, diff_lambda,
                  diff_subln_g, w_out, norm2_g, w_gate, w_up, w_down):
    b, s, _ = x.shape
    sh1, sc1, g1, sh2, sc2, g2 = ada_modulation(c, w_mod, b_mod)

    h = rmsnorm(x, norm1_g) * (1.0 + sc1) + sh1
    proj = jnp.einsum('bsd,de->bse', h, w_in)
    offsets = np.cumsum(IN_SIZES)[:-1].tolist()
    z, xbc, dt_raw, gq, gk, gv, dq, dk, dv = jnp.split(proj, offsets, axis=-1)

    y_ssd = ssd_mixer(z, xbc, dt_raw, conv_w, conv_b, dt_bias, a_log, d_skip, ssd_norm_g)

    gq = axial_rope(rmsnorm(gq.reshape(b, s, GQA_HEADS, GQA_HEAD_DIM), q_norm_g), ang_r, ang_c)
    gk = axial_rope(rmsnorm(gk.reshape(b, s, GQA_KV_HEADS, GQA_HEAD_DIM), k_norm_g), ang_r, ang_c)
    gv = gv.reshape(b, s, GQA_KV_HEADS, GQA_HEAD_DIM)
    y_gqa = gqa_attention(gq, gk, gv)

    lam_init = 0.8 - 0.6 * math.exp(-0.3 * li)
    lp = diff_lambda.astype(F32)
    lam = jnp.exp(jnp.sum(lp[0] * lp[1])) - jnp.exp(jnp.sum(lp[2] * lp[3])) + lam_init
    slopes = 2.0 ** (-8.0 * jnp.arange(1, DIFF_HEADS + 1, dtype=F32) / DIFF_HEADS)
    o = diff_attention(dq.reshape(b, s, DIFF_HEADS, 2, DIFF_QK_DIM),
                       dk.reshape(b, s, DIFF_HEADS, 2, DIFF_QK_DIM),
                       dv.reshape(b, s, DIFF_HEADS, DIFF_V_DIM), lam, slopes)
    y_diff = (rmsnorm(o, diff_subln_g) * (1.0 - lam_init)).reshape(b, s, DIFF_WIDTH)

    mix = jnp.concatenate([y_ssd, y_gqa, y_diff], axis=-1)
    x = x + g1 * jnp.einsum('bse,ed->bsd', mix, w_out)

    h = rmsnorm(x, norm2_g) * (1.0 + sc2) + sh2
    f = jax.nn.silu(jnp.einsum('bsd,df->bsf', h, w_gate)) * jnp.einsum('bsd,df->bsf', h, w_up)
    return x + g2 * jnp.einsum('bsf,fd->bsd', f, w_down)


def encoder(x, c, w_mod, b_mod, norm1_g, w_in, conv_w, conv_b, dt_bias, a_log, d_skip,
            ssd_norm_g, q_norm_g, k_norm_g, diff_lambda, diff_subln_g, w_out, norm2_g,
            w_gate, w_up, w_down, final_g):
    ang_r, ang_c = axial_rope_angles(x.shape[1])
    for li in range(DEPTH):
        x = encoder_layer(x, c, ang_r, ang_c, li, w_mod[li], b_mod[li], norm1_g[li], w_in[li],
                          conv_w[li], conv_b[li], dt_bias[li], a_log[li], d_skip[li],
                          ssd_norm_g[li], q_norm_g[li], k_norm_g[li], diff_lambda[li],
                          diff_subln_g[li], w_out[li], norm2_g[li], w_gate[li], w_up[li],
                          w_down[li])
    return rmsnorm(x, final_g)


def setup_inputs(seed: int = 0) -> dict:
    key = jax.random.key(seed)
    ks = jax.random.split(key, 26)
    nrm = lambda k, shape, scale: jax.random.normal(k, shape, F32) * scale
    gain = lambda k, shape: 1.0 + nrm(k, shape, 0.02)
    dt0 = jnp.exp(jax.random.uniform(ks[9], (DEPTH, 2, SSD_HEADS), F32,
                                     math.log(1e-3), math.log(1e-1)))
    return {
        'x_prompt': nrm(ks[0], (BATCH, SEQ, D_MODEL), 1.0),
        'x_sample': nrm(ks[1], (DEC_BATCH, DEC_SEQ, D_MODEL), 1.0),
        'c_prompt': nrm(ks[2], (BATCH, D_MODEL), 1.0),
        'c_sample': nrm(ks[3], (DEC_BATCH, D_MODEL), 1.0),
        'w_mod': nrm(ks[4], (DEPTH, D_MODEL, 6 * D_MODEL), 0.5 * D_MODEL ** -0.5),
        'b_mod': nrm(ks[5], (DEPTH, 6 * D_MODEL), 0.01),
        'norm1_g': gain(ks[6], (DEPTH, D_MODEL)),
        'w_in': nrm(ks[7], (DEPTH, D_MODEL, IN_COLS), D_MODEL ** -0.5),
        'conv_w': nrm(ks[8], (DEPTH, SSD_CONV_W, SSD_CONV_DIM), SSD_CONV_W ** -0.5),
        'conv_b': nrm(ks[10], (DEPTH, SSD_CONV_DIM), 0.01),
        'dt_bias': dt0 + jnp.log(-jnp.expm1(-dt0)),
        'a_log': jnp.log(jax.random.uniform(ks[11], (DEPTH, 2, SSD_HEADS), F32, 1.0, 16.0)),
        'd_skip': gain(ks[12], (DEPTH, SSD_HEADS)),
        'ssd_norm_g': gain(ks[13], (DEPTH, SSD_WIDTH)),
        'q_norm_g': gain(ks[14], (DEPTH, GQA_HEAD_DIM)),
        'k_norm_g': gain(ks[15], (DEPTH, GQA_HEAD_DIM)),
        'diff_lambda': nrm(ks[16], (DEPTH, 4, DIFF_QK_DIM), 0.1),
        'diff_subln_g': gain(ks[17], (DEPTH, DIFF_V_DIM)),
        'w_out': nrm(ks[18], (DEPTH, MIX_WIDTH, D_MODEL), MIX_WIDTH ** -0.5),
        'norm2_g': gain(ks[19], (DEPTH, D_MODEL)),
        'w_gate': nrm(ks[20], (DEPTH, D_MODEL, D_FF), D_MODEL ** -0.5),
        'w_up': nrm(ks[21], (DEPTH, D_MODEL, D_FF), D_MODEL ** -0.5),
        'w_down': nrm(ks[22], (DEPTH, D_FF, D_MODEL), D_FF ** -0.5),
        'final_g': gain(ks[23], (D_MODEL,)),
    }


def reference(x_prompt, x_sample, c_prompt, c_sample, w_mod, b_mod, norm1_g, w_in, conv_w,
              conv_b, dt_bias, a_log, d_skip, ssd_norm_g, q_norm_g, k_norm_g, diff_lambda,
              diff_subln_g, w_out, norm2_g, w_gate, w_up, w_down, final_g):
    y_prompt = encoder(x_prompt, c_prompt, w_mod, b_mod, norm1_g, w_in, conv_w, conv_b,
                       dt_bias, a_log, d_skip, ssd_norm_g, q_norm_g, k_norm_g, diff_lambda,
                       diff_subln_g, w_out, norm2_g, w_gate, w_up, w_down, final_g)
    y_sample = encoder(x_sample, c_sample, w_mod, b_mod, norm1_g, w_in, conv_w, conv_b,
                       dt_bias, a_log, d_skip, ssd_norm_g, q_norm_g, k_norm_g, diff_lambda,
                       diff_subln_g, w_out, norm2_g, w_gate, w_up, w_down, final_g)
    return (y_prompt, y_sample)
```

```python
import functools
import math

import jax
import jax.numpy as jnp
import numpy as np
from jax import lax
from jax.experimental import pallas as pl
from jax.experimental.pallas import tpu as pltpu

F32 = jnp.float32
BF16 = jnp.bfloat16

NORM_EPS = 1e-6
GRID_W = 64
ROPE_THETA = 10000.0
Q_BLOCK = 128

SSD_HEADS = 16
SSD_HEAD_DIM = 64
SSD_WIDTH = SSD_HEADS * SSD_HEAD_DIM
SSD_GROUPS = 2
SSD_STATE = 128
SSD_CONV_W = 5
SSD_CHUNK = 128
SSD_BC = SSD_GROUPS * SSD_STATE
SSD_CONV_DIM = SSD_WIDTH + 2 * SSD_BC

GQA_HEADS = 4
GQA_KV_HEADS = 2
GQA_HEAD_DIM = 128
GQA_WIDTH = GQA_HEADS * GQA_HEAD_DIM
GQA_KV_WIDTH = GQA_KV_HEADS * GQA_HEAD_DIM

DIFF_HEADS = 4
DIFF_QK_DIM = 64
DIFF_V_DIM = 128
DIFF_QK_WIDTH = DIFF_HEADS * 2 * DIFF_QK_DIM
DIFF_WIDTH = DIFF_HEADS * DIFF_V_DIM

MIX_WIDTH = SSD_WIDTH + GQA_WIDTH + DIFF_WIDTH

LANES = 128
SUBLANES = 8
VMEM_LIMIT_BYTES = 58 * 1024 * 1024

DT_PAD = LANES
SEG_SIZES = dict(z=SSD_WIDTH, xbc=SSD_CONV_DIM, gq=GQA_WIDTH, gk=GQA_KV_WIDTH, gv=GQA_KV_WIDTH,
                 dq=DIFF_QK_WIDTH, dk=DIFF_QK_WIDTH, dv=DIFF_WIDTH, dt=DT_PAD)
SEG_OFF = {}
_o = 0
for _k, _v in SEG_SIZES.items():
    SEG_OFF[_k] = _o
    _o += _v
IN_COLS_PAD = _o

MOD_ROWS = 16


def _cparams(sem):
    return pltpu.CompilerParams(dimension_semantics=sem, vmem_limit_bytes=VMEM_LIMIT_BYTES)


def _silu(x):
    return x * (1.0 / (1.0 + jnp.exp(-x)))


def _mod_kernel(c_ref, w_ref, b_ref, o_ref):
    c = _silu(c_ref[...]).astype(BF16)
    w = w_ref[0].astype(BF16)
    o_ref[0] = jnp.dot(c, w, preferred_element_type=F32) + b_ref[0]


def modulation(c_all, w_mod, b_mod, tn=1024):
    depth, d, n = w_mod.shape
    return pl.pallas_call(
        _mod_kernel,
        grid=(depth, n // tn),
        in_specs=[
            pl.BlockSpec((MOD_ROWS, d), lambda l, j: (0, 0)),
            pl.BlockSpec((1, d, tn), lambda l, j: (l, 0, j)),
            pl.BlockSpec((1, 1, tn), lambda l, j: (l, 0, j)),
        ],
        out_specs=pl.BlockSpec((1, MOD_ROWS, tn), lambda l, j: (l, 0, j)),
        out_shape=jax.ShapeDtypeStruct((depth, MOD_ROWS, n), F32),
        compiler_params=_cparams(("arbitrary", "arbitrary")),
        name="modulation",
    )(c_all, w_mod, b_mod.reshape(depth, 1, n))


def _mod_spec(d, layer, which, row0, tiles_per_seq):
    def imap(i, *_):
        return ((layer * MOD_ROWS + row0 + i // tiles_per_seq) * 6 + which, 0, 0)
    return pl.BlockSpec((1, 1, d), imap)


def _swap32(t):
    lane = lax.broadcasted_iota(jnp.int32, t.shape, 1)
    return jnp.where((lane % 64) < 32, pltpu.roll(t, 96, 1), pltpu.roll(t, 32, 1))


def _inproj_kernel(x_ref, sc_ref, sh_ref, g_ref, w_ref, qg_ref, kg_ref, cos_ref, sin_ref,
                   z_ref, xbc_ref, dt_ref, gq_ref, gk_ref, gv_ref, dq_ref, dk_ref, dv_ref):
    x = x_ref[...]
    inv = lax.rsqrt(jnp.mean(x * x, axis=-1, keepdims=True) + NORM_EPS)
    h = (x * inv * g_ref[...]) * (1.0 + sc_ref[0]) + sh_ref[0]
    hb = h.astype(BF16)

    def proj(name, lo=0, n=None):
        o = SEG_OFF[name] + lo
        n = SEG_SIZES[name] if n is None else n
        return jnp.dot(hb, w_ref[:, o:o + n], preferred_element_type=F32)

    z_ref[...] = proj("z")
    xbc_ref[...] = proj("xbc")
    dt_ref[...] = proj("dt")
    gv_ref[...] = proj("gv").astype(BF16)
    dq_ref[...] = proj("dq").astype(BF16)
    dk_ref[...] = proj("dk").astype(BF16)
    dv_ref[...] = proj("dv").astype(BF16)

    cos = cos_ref[...]
    sin = sin_ref[...]

    def norm_rope(t, gain):
        t = t * lax.rsqrt(jnp.mean(t * t, axis=-1, keepdims=True) + NORM_EPS) * gain
        return t * cos + _swap32(t) * sin

    for hd in range(GQA_HEADS):
        t = proj("gq", hd * GQA_HEAD_DIM, GQA_HEAD_DIM)
        gq_ref[:, hd * GQA_HEAD_DIM:(hd + 1) * GQA_HEAD_DIM] = norm_rope(t, qg_ref[...]).astype(BF16)
    for hd in range(GQA_KV_HEADS):
        t = proj("gk", hd * GQA_HEAD_DIM, GQA_HEAD_DIM)
        gk_ref[:, hd * GQA_HEAD_DIM:(hd + 1) * GQA_HEAD_DIM] = norm_rope(t, kg_ref[...]).astype(BF16)


def in_projection(x2, mod_tab, layer, row0, seq, norm_g, w_in_l, q_g, k_g, cos_tab, sin_tab, tm=512):
    t, d = x2.shape
    tm = min(tm, seq)
    tps = seq // tm
    row = lambda n: pl.BlockSpec((tm, n), lambda i: (i, 0))
    const = lambda shape: pl.BlockSpec(shape, lambda i: (0,) * len(shape))
    pos = pl.BlockSpec((tm, GQA_HEAD_DIM), lambda i: (i % tps, 0))
    outs = [("z", F32), ("xbc", F32), ("dt", F32), ("gq", BF16), ("gk", BF16), ("gv", BF16),
            ("dq", BF16), ("dk", BF16), ("dv", BF16)]
    return pl.pallas_call(
        _inproj_kernel,
        grid=(t // tm,),
        in_specs=[
            row(d),
            _mod_spec(d, layer, 1, row0, tps),
            _mod_spec(d, layer, 0, row0, tps),
            const((1, d)),
            pl.BlockSpec((d, IN_COLS_PAD), lambda i: (0, 0), pipeline_mode=pl.Buffered(1)),
            const((1, GQA_HEAD_DIM)),
            const((1, GQA_HEAD_DIM)),
            pos, pos,
        ],
        out_specs=[row(SEG_SIZES[k]) for k, _ in outs],
        out_shape=[jax.ShapeDtypeStruct((t, SEG_SIZES[k]), dt) for k, dt in outs],
        compiler_params=_cparams(("parallel",)),
        name="in_projection",
    )(x2, mod_tab, mod_tab, norm_g.reshape(1, d), w_in_l, q_g.reshape(1, -1), k_g.reshape(1, -1),
      cos_tab, sin_tab)


def _outproj_kernel(x_ref, ys_ref, yg_ref, yd_ref, w_ref, g_ref, o_ref):
    acc = jnp.dot(ys_ref[...], w_ref[0:SSD_WIDTH, :], preferred_element_type=F32)
    acc += jnp.dot(yg_ref[...], w_ref[SSD_WIDTH:SSD_WIDTH + GQA_WIDTH, :], preferred_element_type=F32)
    acc += jnp.dot(yd_ref[...], w_ref[SSD_WIDTH + GQA_WIDTH:, :], preferred_element_type=F32)
    o_ref[...] = x_ref[...] + g_ref[0] * acc


def out_projection(x2, y_ssd, y_gqa, y_diff, w_out_l, mod_tab, layer, row0, seq, tm=512):
    t, d = x2.shape
    tm = min(tm, seq)
    tps = seq // tm
    row = lambda n: pl.BlockSpec((tm, n), lambda i: (i, 0))
    return pl.pallas_call(
        _outproj_kernel,
        grid=(t // tm,),
        in_specs=[
            row(d), row(SSD_WIDTH), row(GQA_WIDTH), row(DIFF_WIDTH),
            pl.BlockSpec((MIX_WIDTH, d), lambda i: (0, 0), pipeline_mode=pl.Buffered(1)),
            _mod_spec(d, layer, 2, row0, tps),
        ],
        out_specs=row(d),
        out_shape=jax.ShapeDtypeStruct((t, d), F32),
        compiler_params=_cparams(("parallel",)),
        name="out_projection",
    )(x2, y_ssd, y_gqa, y_diff, w_out_l, mod_tab)


def _ffn_kernel(x_ref, sc_ref, sh_ref, gate_ref, g_ref, wg_ref, wu_ref, wd_ref, o_ref, h_ref, acc_ref):
    f = pl.program_id(1)

    @pl.when(f == 0)
    def _():
        x = x_ref[...]
        inv = lax.rsqrt(jnp.mean(x * x, axis=-1, keepdims=True) + NORM_EPS)
        h = (x * inv * g_ref[...]) * (1.0 + sc_ref[0]) + sh_ref[0]
        h_ref[...] = h.astype(BF16)

    hb = h_ref[...]
    a = jnp.dot(hb, wg_ref[...], preferred_element_type=F32)
    b = jnp.dot(hb, wu_ref[...], preferred_element_type=F32)
    act = (_silu(a) * b).astype(BF16)
    part = jnp.dot(act, wd_ref[...], preferred_element_type=F32)

    @pl.when(f == 0)
    def _():
        acc_ref[...] = part

    @pl.when(f > 0)
    def _():
        acc_ref[...] += part

    @pl.when(f == pl.num_programs(1) - 1)
    def _():
        o_ref[...] = x_ref[...] + gate_ref[0] * acc_ref[...]


def ffn(x2, mod_tab, layer, row0, seq, norm_g, wg, wu, wd, tm=512, tf=512):
    t, d = x2.shape
    dff = wg.shape[1]
    tm = min(tm, seq)
    tps = seq // tm
    return pl.pallas_call(
        _ffn_kernel,
        grid=(t // tm, dff // tf),
        in_specs=[
            pl.BlockSpec((tm, d), lambda i, f: (i, 0)),
            _mod_spec(d, layer, 4, row0, tps),
            _mod_spec(d, layer, 3, row0, tps),
            _mod_spec(d, layer, 5, row0, tps),
            pl.BlockSpec((1, d), lambda i, f: (0, 0)),
            pl.BlockSpec((d, tf), lambda i, f: (0, f)),
            pl.BlockSpec((d, tf), lambda i, f: (0, f)),
            pl.BlockSpec((tf, d), lambda i, f: (f, 0)),
        ],
        out_specs=pl.BlockSpec((tm, d), lambda i, f: (i, 0)),
        out_shape=jax.ShapeDtypeStruct((t, d), F32),
        scratch_shapes=[pltpu.VMEM((tm, d), BF16), pltpu.VMEM((tm, d), F32)],
        compiler_params=_cparams(("parallel", "arbitrary")),
        name="ffn",
    )(x2, mod_tab, mod_tab, mod_tab, norm_g.reshape(1, d), wg, wu, wd)


def _final_norm_kernel(x_ref, g_ref, o_ref):
    x = x_ref[...]
    o_ref[...] = x * lax.rsqrt(jnp.mean(x * x, axis=-1, keepdims=True) + NORM_EPS) * g_ref[...]


def final_norm(x2, g, tm=512):
    t, d = x2.shape
    tm = min(tm, t)
    return pl.pallas_call(
        _final_norm_kernel,
        grid=(t // tm,),
        in_specs=[pl.BlockSpec((tm, d), lambda i: (i, 0)), pl.BlockSpec((1, d), lambda i: (0, 0))],
        out_specs=pl.BlockSpec((tm, d), lambda i: (i, 0)),
        out_shape=jax.ShapeDtypeStruct((t, d), F32),
        compiler_params=_cparams(("parallel",)),
        name="final_norm",
    )(x2, g.reshape(1, d))


def _segsum_exp(a):
    cs = jnp.cumsum(a, axis=-1)
    diff = cs[..., :, None] - cs[..., None, :]
    n = a.shape[-1]
    mask = jnp.tril(jnp.ones((n, n), dtype=bool))
    return jnp.exp(jnp.where(mask, diff, -jnp.inf))


def _ssd_scan(x, dt, a, bm, cm):
    b, s, h, p = x.shape
    g, n = bm.shape[-2:]
    r = h // g
    nc, l = s // SSD_CHUNK, SSD_CHUNK
    xc = (x * dt[..., None]).reshape(b, nc, l, g, r, p)
    ac = jnp.moveaxis((dt * a).reshape(b, nc, l, g, r), 2, -1)
    bc = bm.reshape(b, nc, l, g, n)
    cc = cm.reshape(b, nc, l, g, n)
    a_cs = jnp.cumsum(ac, axis=-1)
    decay = _segsum_exp(ac)
    scores = jnp.einsum('bclgn,bcsgn->bcgls', cc, bc)
    y_diag = jnp.einsum('bcgrls,bcsgrp->bclgrp', scores[:, :, :, None] * decay, xc)
    decay_to_end = jnp.moveaxis(jnp.exp(a_cs[..., -1:] - a_cs), -1, 2)
    chunk_states = jnp.einsum('bcsgn,bcsgrp->bcgrpn', bc, xc * decay_to_end[..., None])
    chunk_decay = jnp.exp(a_cs[..., -1])

    def step(carry, inp):
        st, dec = inp
        return carry * dec[..., None, None] + st, carry

    init = jnp.zeros((b, g, r, p, n), dtype=chunk_states.dtype)
    _, prev = lax.scan(step, init, (jnp.moveaxis(chunk_states, 1, 0), jnp.moveaxis(chunk_decay, 1, 0)))
    prev = jnp.moveaxis(prev, 0, 1)
    decay_in = jnp.moveaxis(jnp.exp(a_cs), -1, 2)
    y_off = jnp.einsum('bclgn,bcgrpn->bclgrp', cc, prev) * decay_in[..., None]
    return (y_diag + y_off).reshape(b, s, h, p)


def _ssd_mixer_jnp(z, xbc, dt_raw, conv_w, conv_b, dt_bias, a_log, d_skip, norm_g):
    b, s, _ = xbc.shape
    k, ch = conv_w.shape
    pad = k // 2
    y = lax.conv_general_dilated(
        xbc, conv_w[:, None, :], window_strides=(1,), padding=[(pad, pad)],
        dimension_numbers=('NWC', 'WIO', 'NWC'), feature_group_count=ch)
    xbc = jax.nn.silu(y + conv_b)
    xs, bm, cm = jnp.split(xbc, [SSD_WIDTH, SSD_WIDTH + SSD_BC], axis=-1)
    xs = xs.reshape(b, s, SSD_HEADS, SSD_HEAD_DIM)
    bm = bm.reshape(b, s, SSD_GROUPS, SSD_STATE)
    cm = cm.reshape(b, s, SSD_GROUPS, SSD_STATE)
    dt = jax.nn.softplus(dt_raw.reshape(b, s, 2, SSD_HEADS) + dt_bias)
    a = -jnp.exp(a_log)
    y_fwd = _ssd_scan(xs, dt[:, :, 0], a[0], bm, cm)
    flip = lambda t: jnp.flip(t, axis=1)
    y_bwd = flip(_ssd_scan(flip(xs), flip(dt[:, :, 1]), a[1], flip(bm), flip(cm)))
    y = y_fwd + y_bwd + xs * d_skip[:, None]
    y = y.reshape(b, s, SSD_WIDTH) * jax.nn.silu(z)
    yg = y.reshape(b, s, SSD_GROUPS, SSD_WIDTH // SSD_GROUPS)
    yg = yg * lax.rsqrt(jnp.mean(yg * yg, axis=-1, keepdims=True) + NORM_EPS)
    return yg.reshape(b, s, SSD_WIDTH) * norm_g


def _gqa_jnp(q, k, v):
    b, s, hq, d = q.shape
    r = hq // GQA_KV_HEADS
    nb = s // Q_BLOCK
    qb = jnp.moveaxis(q.reshape(b, nb, Q_BLOCK, GQA_KV_HEADS, r, d), 1, 0)
    scale = d ** -0.5

    def block(qblk):
        sc = jnp.einsum('blgrd,bsgd->bgrls', qblk, k, preferred_element_type=F32) * scale
        p = jax.nn.softmax(sc, axis=-1)
        return jnp.einsum('bgrls,bsgd->blgrd', p.astype(v.dtype), v, preferred_element_type=F32)

    o = lax.map(block, qb)
    return jnp.moveaxis(o, 0, 1).reshape(b, s, hq * d)


def _diff_jnp(q, k, v, lam, slopes):
    b, s, h, _, dk = q.shape
    nb = s // Q_BLOCK
    qb = jnp.moveaxis(q.reshape(b, nb, Q_BLOCK, h, 2, dk), 1, 0)
    kpos = jnp.arange(s, dtype=F32)
    qpos = kpos.reshape(nb, Q_BLOCK)
    scale = dk ** -0.5

    def block(args):
        qblk, qp = args
        sc = jnp.einsum('blhmd,bshmd->bhmls', qblk, k, preferred_element_type=F32) * scale
        alibi = -slopes[:, None, None] * jnp.abs(qp[:, None] - kpos[None, :])
        p = jax.nn.softmax(sc + alibi[None, :, None], axis=-1)
        attn = p[:, :, 0] - lam * p[:, :, 1]
        return jnp.einsum('bhls,bshe->blhe', attn.astype(v.dtype), v, preferred_element_type=F32)

    o = lax.map(block, (qb, qpos))
    return jnp.moveaxis(o, 0, 1).reshape(b, s, h, v.shape[-1])


def _rope_tables(seq):
    pos = np.arange(seq)
    row_idx = (pos // GRID_W).astype(np.float32)
    col_idx = (pos % GRID_W).astype(np.float32)
    axis_dim = GQA_HEAD_DIM // 2
    inv_freq = jnp.asarray(ROPE_THETA, F32) ** (-jnp.arange(0, axis_dim, 2, dtype=F32) / axis_dim)
    ang_r = jnp.asarray(row_idx)[:, None] * inv_freq
    ang_c = jnp.asarray(col_idx)[:, None] * inv_freq
    cr, sr, cc, sc = jnp.cos(ang_r), jnp.sin(ang_r), jnp.cos(ang_c), jnp.sin(ang_c)
    cos_tab = jnp.concatenate([cr, cr, cc, cc], axis=-1)
    sin_tab = jnp.concatenate([-sr, sr, -sc, sc], axis=-1)
    return cos_tab, sin_tab


def _relayout_w_in(w_in):
    ref_sizes = [("z", SSD_WIDTH), ("xbc", SSD_CONV_DIM), ("dt", 2 * SSD_HEADS), ("gq", GQA_WIDTH),
                 ("gk", GQA_KV_WIDTH), ("gv", GQA_KV_WIDTH), ("dq", DIFF_QK_WIDTH),
                 ("dk", DIFF_QK_WIDTH), ("dv", DIFF_WIDTH)]
    parts, o = {}, 0
    for k, n in ref_sizes:
        parts[k] = w_in[:, :, o:o + n]
        o += n
    parts["dt"] = jnp.pad(parts["dt"], ((0, 0), (0, 0), (0, DT_PAD - 2 * SSD_HEADS)))
    return jnp.concatenate([parts[k] for k in SEG_SIZES], axis=-1).astype(BF16)


def _encoder_group(x, mod_tab, row0, weights, final_g):
    b, s, d = x.shape
    depth = weights["w_in"].shape[0]
    x2 = x.reshape(b * s, d)
    cos_tab, sin_tab = _rope_tables(s)
    for li in range(depth):
        w = {k: v[li] for k, v in weights.items()}
        z, xbc, dtr, gq, gk, gv, dq, dk, dv = in_projection(
            x2, mod_tab, li, row0, s, w["norm1_g"], w["w_in"], w["q_norm_g"], w["k_norm_g"],
            cos_tab, sin_tab)

        y_ssd = _ssd_mixer_jnp(z.reshape(b, s, -1), xbc.reshape(b, s, -1),
                               dtr[:, :2 * SSD_HEADS].reshape(b, s, -1), w["conv_w"], w["conv_b"],
                               w["dt_bias"], w["a_log"], w["d_skip"], w["ssd_norm_g"])
        y_gqa = _gqa_jnp(gq.reshape(b, s, GQA_HEADS, GQA_HEAD_DIM),
                         gk.reshape(b, s, GQA_KV_HEADS, GQA_HEAD_DIM),
                         gv.reshape(b, s, GQA_KV_HEADS, GQA_HEAD_DIM))
        lam_init = 0.8 - 0.6 * math.exp(-0.3 * li)
        lp = w["diff_lambda"]
        lam = jnp.exp(jnp.sum(lp[0] * lp[1])) - jnp.exp(jnp.sum(lp[2] * lp[3])) + lam_init
        slopes = 2.0 ** (-8.0 * jnp.arange(1, DIFF_HEADS + 1, dtype=F32) / DIFF_HEADS)
        o = _diff_jnp(dq.reshape(b, s, DIFF_HEADS, 2, DIFF_QK_DIM),
                      dk.reshape(b, s, DIFF_HEADS, 2, DIFF_QK_DIM),
                      dv.reshape(b, s, DIFF_HEADS, DIFF_V_DIM), lam, slopes)
        o = o * lax.rsqrt(jnp.mean(o * o, axis=-1, keepdims=True) + NORM_EPS) * w["diff_subln_g"]
        y_diff = (o * (1.0 - lam_init)).reshape(b * s, DIFF_WIDTH)

        x2 = out_projection(x2, y_ssd.reshape(b * s, -1).astype(BF16), y_gqa.reshape(b * s, -1).astype(BF16),
                            y_diff.astype(BF16), w["w_out"], mod_tab, li, row0, s)
        x2 = ffn(x2, mod_tab, li, row0, s, w["norm2_g"], w["w_gate"], w["w_up"], w["w_down"])
    return final_norm(x2, final_g).reshape(b, s, d)


def kernel(x_prompt, x_sample, c_prompt, c_sample, w_mod, b_mod, norm1_g, w_in, conv_w, conv_b, dt_bias,
           a_log, d_skip, ssd_norm_g, q_norm_g, k_norm_g, diff_lambda, diff_subln_g, w_out, norm2_g,
           w_gate, w_up, w_down, final_g):
    depth, d, _ = w_mod.shape
    bp, bs = c_prompt.shape[0], c_sample.shape[0]
    assert bp + bs <= MOD_ROWS
    c_all = jnp.concatenate([c_prompt, c_sample, jnp.zeros((MOD_ROWS - bp - bs, d), F32)], axis=0)
    mod = modulation(c_all, w_mod, b_mod)
    mod_tab = mod.reshape(depth * MOD_ROWS * 6, 1, d)

    weights = dict(
        norm1_g=norm1_g, w_in=_relayout_w_in(w_in), conv_w=conv_w, conv_b=conv_b, dt_bias=dt_bias,
        a_log=a_log, d_skip=d_skip, ssd_norm_g=ssd_norm_g, q_norm_g=q_norm_g, k_norm_g=k_norm_g,
        diff_lambda=diff_lambda, diff_subln_g=diff_subln_g, w_out=w_out.astype(BF16), norm2_g=norm2_g,
        w_gate=w_gate.astype(BF16), w_up=w_up.astype(BF16), w_down=w_down.astype(BF16))
    y_prompt = _encoder_group(x_prompt, mod_tab, 0, weights, final_g)
    y_sample = _encoder_group(x_sample, mod_tab, bp, weights, final_g)
    return (y_prompt, y_sample)
```

```python
import functools
import math

import jax
import jax.numpy as jnp
import numpy as np
from jax import lax
from jax.experimental import pallas as pl
from jax.experimental.pallas import tpu as pltpu

F32 = jnp.float32
BF16 = jnp.bfloat16

NORM_EPS = 1e-6
GRID_W = 64
ROPE_THETA = 10000.0
Q_BLOCK = 128

SSD_HEADS = 16
SSD_HEAD_DIM = 64
SSD_WIDTH = SSD_HEADS * SSD_HEAD_DIM
SSD_GROUPS = 2
SSD_STATE = 128
SSD_CONV_W = 5
SSD_CHUNK = 128
SSD_BC = SSD_GROUPS * SSD_STATE
SSD_CONV_DIM = SSD_WIDTH + 2 * SSD_BC

GQA_HEADS = 4
GQA_KV_HEADS = 2
GQA_HEAD_DIM = 128
GQA_WIDTH = GQA_HEADS * GQA_HEAD_DIM
GQA_KV_WIDTH = GQA_KV_HEADS * GQA_HEAD_DIM

DIFF_HEADS = 4
DIFF_QK_DIM = 64
DIFF_V_DIM = 128
DIFF_QK_WIDTH = DIFF_HEADS * 2 * DIFF_QK_DIM
DIFF_WIDTH = DIFF_HEADS * DIFF_V_DIM

MIX_WIDTH = SSD_WIDTH + GQA_WIDTH + DIFF_WIDTH

LANES = 128
SUBLANES = 8
VMEM_LIMIT_BYTES = 58 * 1024 * 1024

DT_PAD = LANES
SEG_SIZES = dict(z=SSD_WIDTH, xbc=SSD_CONV_DIM, gq=GQA_WIDTH, gk=GQA_KV_WIDTH, gv=GQA_KV_WIDTH,
                 dq=DIFF_QK_WIDTH, dk=DIFF_QK_WIDTH, dv=DIFF_WIDTH, dt=DT_PAD)
SEG_OFF = {}
_o = 0
for _k, _v in SEG_SIZES.items():
    SEG_OFF[_k] = _o
    _o += _v
IN_COLS_PAD = _o

MOD_ROWS = 16


def _cparams(sem):
    return pltpu.CompilerParams(dimension_semantics=sem, vmem_limit_bytes=VMEM_LIMIT_BYTES)


def _silu(x):
    return x * (1.0 / (1.0 + jnp.exp(-x)))


def _mod_kernel(c_ref, w_ref, b_ref, o_ref):
    c = _silu(c_ref[...]).astype(BF16)
    w = w_ref[0].astype(BF16)
    o_ref[0] = jnp.dot(c, w, preferred_element_type=F32) + b_ref[0]


def modulation(c_all, w_mod, b_mod, tn=1024):
    depth, d, n = w_mod.shape
    return pl.pallas_call(
        _mod_kernel,
        grid=(depth, n // tn),
        in_specs=[
            pl.BlockSpec((MOD_ROWS, d), lambda l, j: (0, 0)),
            pl.BlockSpec((1, d, tn), lambda l, j: (l, 0, j)),
            pl.BlockSpec((1, 1, tn), lambda l, j: (l, 0, j)),
        ],
        out_specs=pl.BlockSpec((1, MOD_ROWS, tn), lambda l, j: (l, 0, j)),
        out_shape=jax.ShapeDtypeStruct((depth, MOD_ROWS, n), F32),
        compiler_params=_cparams(("arbitrary", "arbitrary")),
        name="modulation",
    )(c_all, w_mod, b_mod.reshape(depth, 1, n))


def _mod_spec(d, layer, which, row0, tiles_per_seq):
    def imap(i, *_):
        return ((layer * MOD_ROWS + row0 + i // tiles_per_seq) * 6 + which, 0, 0)
    return pl.BlockSpec((1, 1, d), imap)


def _swap32(t):
    lane = lax.broadcasted_iota(jnp.int32, t.shape, 1)
    return jnp.where((lane % 64) < 32, pltpu.roll(t, 96, 1), pltpu.roll(t, 32, 1))


def _inproj_kernel(x_ref, sc_ref, sh_ref, g_ref, w_ref, qg_ref, kg_ref, cos_ref, sin_ref,
                   z_ref, xbc_ref, dt_ref, gq_ref, gk_ref, gv_ref, dq_ref, dk_ref, dv_ref):
    x = x_ref[...]
    inv = lax.rsqrt(jnp.mean(x * x, axis=-1, keepdims=True) + NORM_EPS)
    h = (x * inv * g_ref[...]) * (1.0 + sc_ref[0]) + sh_ref[0]
    hb = h.astype(BF16)

    def proj(name, lo=0, n=None):
        o = SEG_OFF[name] + lo
        n = SEG_SIZES[name] if n is None else n
        return jnp.dot(hb, w_ref[:, o:o + n], preferred_element_type=F32)

    z_ref[...] = proj("z")
    xbc_ref[...] = proj("xbc")
    dt_ref[...] = proj("dt")
    gv_ref[...] = proj("gv").astype(BF16)
    dq_ref[...] = proj("dq").astype(BF16)
    dk_ref[...] = proj("dk").astype(BF16)
    dv_ref[...] = proj("dv").astype(BF16)

    cos = cos_ref[...]
    sin = sin_ref[...]

    def norm_rope(t, gain):
        t = t * lax.rsqrt(jnp.mean(t * t, axis=-1, keepdims=True) + NORM_EPS) * gain
        return t * cos + _swap32(t) * sin

    for hd in range(GQA_HEADS):
        t = proj("gq", hd * GQA_HEAD_DIM, GQA_HEAD_DIM)
        gq_ref[:, hd * GQA_HEAD_DIM:(hd + 1) * GQA_HEAD_DIM] = norm_rope(t, qg_ref[...]).astype(BF16)
    for hd in range(GQA_KV_HEADS):
        t = proj("gk", hd * GQA_HEAD_DIM, GQA_HEAD_DIM)
        gk_ref[:, hd * GQA_HEAD_DIM:(hd + 1) * GQA_HEAD_DIM] = norm_rope(t, kg_ref[...]).astype(BF16)


def in_projection(x2, mod_tab, layer, row0, seq, norm_g, w_in_l, q_g, k_g, cos_tab, sin_tab, tm=512):
    t, d = x2.shape
    tm = min(tm, seq)
    tps = seq // tm
    row = lambda n: pl.BlockSpec((tm, n), lambda i: (i, 0))
    const = lambda shape: pl.BlockSpec(shape, lambda i: (0,) * len(shape))
    pos = pl.BlockSpec((tm, GQA_HEAD_DIM), lambda i: (i % tps, 0))
    outs = [("z", F32), ("xbc", F32), ("dt", F32), ("gq", BF16), ("gk", BF16), ("gv", BF16),
            ("dq", BF16), ("dk", BF16), ("dv", BF16)]
    return pl.pallas_call(
        _inproj_kernel,
        grid=(t // tm,),
        in_specs=[
            row(d),
            _mod_spec(d, layer, 1, row0, tps),
            _mod_spec(d, layer, 0, row0, tps),
            const((1, d)),
            pl.BlockSpec((d, IN_COLS_PAD), lambda i: (0, 0), pipeline_mode=pl.Buffered(1)),
            const((1, GQA_HEAD_DIM)),
            const((1, GQA_HEAD_DIM)),
            pos, pos,
        ],
        out_specs=[row(SEG_SIZES[k]) for k, _ in outs],
        out_shape=[jax.ShapeDtypeStruct((t, SEG_SIZES[k]), dt) for k, dt in outs],
        compiler_params=_cparams(("parallel",)),
        name="in_projection",
    )(x2, mod_tab, mod_tab, norm_g.reshape(1, d), w_in_l, q_g.reshape(1, -1), k_g.reshape(1, -1),
      cos_tab, sin_tab)


def _outproj_kernel(x_ref, ys_ref, yg_ref, yd_ref, w_ref, g_ref, o_ref):
    acc = jnp.dot(ys_ref[...], w_ref[0:SSD_WIDTH, :], preferred_element_type=F32)
    acc += jnp.dot(yg_ref[...], w_ref[SSD_WIDTH:SSD_WIDTH + GQA_WIDTH, :], preferred_element_type=F32)
    acc += jnp.dot(yd_ref[...], w_ref[SSD_WIDTH + GQA_WIDTH:, :], preferred_element_type=F32)
    o_ref[...] = x_ref[...] + g_ref[0] * acc


def out_projection(x2, y_ssd, y_gqa, y_diff, w_out_l, mod_tab, layer, row0, seq, tm=512):
    t, d = x2.shape
    tm = min(tm, seq)
    tps = seq // tm
    row = lambda n: pl.BlockSpec((tm, n), lambda i: (i, 0))
    return pl.pallas_call(
        _outproj_kernel,
        grid=(t // tm,),
        in_specs=[
            row(d), row(SSD_WIDTH), row(GQA_WIDTH), row(DIFF_WIDTH),
            pl.BlockSpec((MIX_WIDTH, d), lambda i: (0, 0), pipeline_mode=pl.Buffered(1)),
            _mod_spec(d, layer, 2, row0, tps),
        ],
        out_specs=row(d),
        out_shape=jax.ShapeDtypeStruct((t, d), F32),
        compiler_params=_cparams(("parallel",)),
        name="out_projection",
    )(x2, y_ssd, y_gqa, y_diff, w_out_l, mod_tab)


def _ffn_kernel(x_ref, sc_ref, sh_ref, gate_ref, g_ref, wg_ref, wu_ref, wd_ref, o_ref, h_ref, acc_ref):
    f = pl.program_id(1)

    @pl.when(f == 0)
    def _():
        x = x_ref[...]
        inv = lax.rsqrt(jnp.mean(x * x, axis=-1, keepdims=True) + NORM_EPS)
        h = (x * inv * g_ref[...]) * (1.0 + sc_ref[0]) + sh_ref[0]
        h_ref[...] = h.astype(BF16)

    hb = h_ref[...]
    a = jnp.dot(hb, wg_ref[...], preferred_element_type=F32)
    b = jnp.dot(hb, wu_ref[...], preferred_element_type=F32)
    act = (_silu(a) * b).astype(BF16)
    part = jnp.dot(act, wd_ref[...], preferred_element_type=F32)

    @pl.when(f == 0)
    def _():
        acc_ref[...] = part

    @pl.when(f > 0)
    def _():
        acc_ref[...] += part

    @pl.when(f == pl.num_programs(1) - 1)
    def _():
        o_ref[...] = x_ref[...] + gate_ref[0] * acc_ref[...]


def ffn(x2, mod_tab, layer, row0, seq, norm_g, wg, wu, wd, tm=512, tf=512):
    t, d = x2.shape
    dff = wg.shape[1]
    tm = min(tm, seq)
    tps = seq // tm
    return pl.pallas_call(
        _ffn_kernel,
        grid=(t // tm, dff // tf),
        in_specs=[
            pl.BlockSpec((tm, d), lambda i, f: (i, 0)),
            _mod_spec(d, layer, 4, row0, tps),
            _mod_spec(d, layer, 3, row0, tps),
            _mod_spec(d, layer, 5, row0, tps),
            pl.BlockSpec((1, d), lambda i, f: (0, 0)),
            pl.BlockSpec((d, tf), lambda i, f: (0, f)),
            pl.BlockSpec((d, tf), lambda i, f: (0, f)),
            pl.BlockSpec((tf, d), lambda i, f: (f, 0)),
        ],
        out_specs=pl.BlockSpec((tm, d), lambda i, f: (i, 0)),
        out_shape=jax.ShapeDtypeStruct((t, d), F32),
        scratch_shapes=[pltpu.VMEM((tm, d), BF16), pltpu.VMEM((tm, d), F32)],
        compiler_params=_cparams(("parallel", "arbitrary")),
        name="ffn",
    )(x2, mod_tab, mod_tab, mod_tab, norm_g.reshape(1, d), wg, wu, wd)


def _final_norm_kernel(x_ref, g_ref, o_ref):
    x = x_ref[...]
    o_ref[...] = x * lax.rsqrt(jnp.mean(x * x, axis=-1, keepdims=True) + NORM_EPS) * g_ref[...]


def final_norm(x2, g, tm=512):
    t, d = x2.shape
    tm = math.gcd(tm, t)
    return pl.pallas_call(
        _final_norm_kernel,
        grid=(t // tm,),
        in_specs=[pl.BlockSpec((tm, d), lambda i: (i, 0)), pl.BlockSpec((1, d), lambda i: (0, 0))],
        out_specs=pl.BlockSpec((tm, d), lambda i: (i, 0)),
        out_shape=jax.ShapeDtypeStruct((t, d), F32),
        compiler_params=_cparams(("parallel",)),
        name="final_norm",
    )(x2, g.reshape(1, d))


LOG2E = 1.4426950408889634


def _attend(s2, v):
    m = jnp.max(s2, axis=-1, keepdims=True)
    e = jnp.exp2(s2 - m)
    l = jnp.sum(e, axis=-1, keepdims=True)
    o = jnp.dot(e.astype(BF16), v, preferred_element_type=F32)
    return o * (1.0 / l)


def _qk(q, k):
    return lax.dot_general(q, k, (((1,), (1,)), ((), ())), preferred_element_type=F32)


def _gqa_kernel(q_ref, k_ref, v_ref, o_ref):
    tq = q_ref.shape[0]
    d = GQA_HEAD_DIM
    q = jnp.concatenate([q_ref[:, 0:d], q_ref[:, d:2 * d]], axis=0)
    s2 = _qk(q, k_ref[...]) * (LOG2E * d ** -0.5)
    o = _attend(s2, v_ref[...])
    o_ref[:, 0:d] = o[0:tq].astype(o_ref.dtype)
    o_ref[:, d:2 * d] = o[tq:2 * tq].astype(o_ref.dtype)


def _attn_tile(seq):
    return min(seq, max(128, (512 * 1024) // seq))


def gqa_attention(gq, gk, gv, batch, seq):
    t = gq.shape[0]
    tq = _attn_tile(seq)
    nq = seq // tq
    rep = GQA_HEADS // GQA_KV_HEADS
    w = rep * GQA_HEAD_DIM
    return pl.pallas_call(
        _gqa_kernel,
        grid=(batch, GQA_KV_HEADS, nq),
        in_specs=[
            pl.BlockSpec((tq, w), lambda b, g, i: (b * nq + i, g)),
            pl.BlockSpec((seq, GQA_HEAD_DIM), lambda b, g, i: (b, g)),
            pl.BlockSpec((seq, GQA_HEAD_DIM), lambda b, g, i: (b, g)),
        ],
        out_specs=pl.BlockSpec((tq, w), lambda b, g, i: (b * nq + i, g)),
        out_shape=jax.ShapeDtypeStruct((t, GQA_WIDTH), BF16),
        compiler_params=_cparams(("parallel", "parallel", "arbitrary")),
        name="gqa_attention",
    )(gq, gk, gv)


def _diff_kernel(lam_init, q_ref, k_ref, v_ref, lp_ref, g_ref, o_ref):
    tq = q_ref.shape[0]
    seq = k_ref.shape[0]
    hd = pl.program_id(1)
    q0 = pl.program_id(2) * tq
    q = q_ref[...] * jnp.asarray(DIFF_QK_DIM ** -0.5, BF16)
    lane = lax.broadcasted_iota(jnp.int32, q.shape, 1)
    zero = jnp.zeros_like(q)
    q12 = jnp.concatenate([jnp.where(lane < DIFF_QK_DIM, q, zero), jnp.where(lane >= DIFF_QK_DIM, q, zero)], axis=0)
    s = _qk(q12, k_ref[...])
    slope = jnp.exp2(-8.0 * (jnp.full((1, 1), hd, jnp.int32).astype(F32) + 1.0) / DIFF_HEADS)
    rel = (lax.broadcasted_iota(jnp.int32, (tq, seq), 1) - lax.broadcasted_iota(jnp.int32, (tq, seq), 0) - q0)
    bias = jnp.abs(rel).astype(F32) * (-LOG2E * slope)
    s2 = s * LOG2E + jnp.concatenate([bias, bias], axis=0)
    o = _attend(s2, v_ref[...])
    lp = lp_ref[...]
    lam = (jnp.exp(jnp.sum(lp[0:1] * lp[1:2], axis=-1, keepdims=True))
           - jnp.exp(jnp.sum(lp[2:3] * lp[3:4], axis=-1, keepdims=True)) + lam_init)
    o = o[0:tq] - lam * o[tq:2 * tq]
    o = o * lax.rsqrt(jnp.mean(o * o, axis=-1, keepdims=True) + NORM_EPS) * g_ref[...]
    o_ref[...] = (o * (1.0 - lam_init)).astype(o_ref.dtype)


def diff_attention(dq, dk, dv, diff_lambda, subln_g, lam_init, batch, seq):
    t = dq.shape[0]
    tq = _attn_tile(seq)
    nq = seq // tq
    w = DIFF_V_DIM
    return pl.pallas_call(
        functools.partial(_diff_kernel, lam_init),
        grid=(batch, DIFF_HEADS, nq),
        in_specs=[
            pl.BlockSpec((tq, w), lambda b, h, i: (b * nq + i, h)),
            pl.BlockSpec((seq, w), lambda b, h, i: (b, h)),
            pl.BlockSpec((seq, w), lambda b, h, i: (b, h)),
            pl.BlockSpec((4, DIFF_QK_DIM), lambda b, h, i: (0, 0)),
            pl.BlockSpec((1, w), lambda b, h, i: (0, 0)),
        ],
        out_specs=pl.BlockSpec((tq, w), lambda b, h, i: (b * nq + i, h)),
        out_shape=jax.ShapeDtypeStruct((t, DIFF_WIDTH), BF16),
        compiler_params=_cparams(("parallel", "parallel", "arbitrary")),
        name="diff_attention",
    )(dq, dk, dv, diff_lambda, subln_g.reshape(1, w))


HALO = SUBLANES
HEADS_PER_GROUP = SSD_HEADS // SSD_GROUPS
GROUP_WIDTH = SSD_WIDTH // SSD_GROUPS


def _split3(x):
    hi = x.astype(BF16)
    r1 = x - hi.astype(F32)
    mid = r1.astype(BF16)
    lo = (r1 - mid.astype(F32)).astype(BF16)
    return hi, mid, lo


def _dot_sel_rhs(x, sel):
    hi, mid, lo = _split3(x)
    return (jnp.dot(hi, sel, preferred_element_type=F32) + jnp.dot(mid, sel, preferred_element_type=F32)
            + jnp.dot(lo, sel, preferred_element_type=F32))


def _dot_sel_lhs(sel, x):
    hi, mid, lo = _split3(x)
    return (jnp.dot(sel, hi, preferred_element_type=F32) + jnp.dot(sel, mid, preferred_element_type=F32)
            + jnp.dot(sel, lo, preferred_element_type=F32))


def _softplus(x):
    return jnp.maximum(x, 0.0) + jnp.log1p(jnp.exp(-jnp.abs(x)))


def _ssd_chunk(xb, bm, cm, dt, da, tri, mask, expand, state_ref, lane0, reverse):
    L = SSD_CHUNK
    acs = _dot_sel_lhs(tri, da)
    acs_t = acs.T
    dt_t = dt.T
    tot = acs[0:1, :] if reverse else acs[L - 1:L, :]
    w = dt * jnp.exp(tot - acs)
    din = jnp.exp(acs)
    wx = _dot_sel_rhs(jnp.concatenate([w, din], axis=0), expand)
    w_x, din_x = wx[0:L], wx[L:2 * L]
    cd_x = din_x[0:1] if reverse else din_x[L - 1:L]
    xw = (xb.astype(F32) * w_x).astype(BF16)
    lane = lax.broadcasted_iota(jnp.int32, (L, 2 * SSD_HEAD_DIM), 1)
    ys = []
    for g in range(SSD_GROUPS):
        bg = bm[:, g * SSD_STATE:(g + 1) * SSD_STATE]
        cg = cm[:, g * SSD_STATE:(g + 1) * SSD_STATE]
        scores = _qk(cg, bg)
        prev = state_ref[g]
        y_off = jnp.dot(cg, prev.astype(BF16), preferred_element_type=F32)
        cols = slice(g * GROUP_WIDTH, (g + 1) * GROUP_WIDTH)
        y_g = y_off * din_x[:, cols]
        pieces = []
        for pr in range(HEADS_PER_GROUP // 2):
            c0 = g * GROUP_WIDTH + pr * 2 * SSD_HEAD_DIM
            xp = xb[:, c0:c0 + 2 * SSD_HEAD_DIM]
            acc = None
            for half in range(2):
                row = lane0 + g * HEADS_PER_GROUP + 2 * pr + half
                diff = acs[:, row:row + 1] - acs_t[row:row + 1, :]
                m = scores * jnp.exp(jnp.where(mask, diff, -jnp.inf)) * dt_t[row:row + 1, :]
                keep = (lane < SSD_HEAD_DIM) if half == 0 else (lane >= SSD_HEAD_DIM)
                xh = jnp.where(keep, xp, jnp.zeros_like(xp))
                part = jnp.dot(m.astype(BF16), xh, preferred_element_type=F32)
                acc = part if acc is None else acc + part
            pieces.append(acc)
        ys.append(y_g + jnp.concatenate(pieces, axis=1))
        new = lax.dot_general(bg, xw[:, cols], (((0,), (0,)), ((), ())), preferred_element_type=F32)
        state_ref[g] = prev * cd_x[:, cols] + new
    return jnp.concatenate(ys, axis=1)


def _chunk_mask(reverse):
    r = lax.broadcasted_iota(jnp.int32, (SSD_CHUNK, SSD_CHUNK), 0)
    c = lax.broadcasted_iota(jnp.int32, (SSD_CHUNK, SSD_CHUNK), 1)
    return (c >= r) if reverse else (r >= c)


def _ssd_fwd_kernel(xm_ref, xprev_ref, xnext_ref, dtr_ref, cw_ref, cb_ref, dtb_ref, alog_ref, dskip_ref,
                    tri_ref, exp_ref, xc_ref, yf_ref, xpad_ref, state_ref):
    i = pl.program_id(1)
    n = pl.num_programs(1)
    rows = xm_ref.shape[0]

    @pl.when(i == 0)
    def _():
        state_ref[...] = jnp.zeros_like(state_ref)

    xpad_ref[0:HALO, :] = jnp.where(i > 0, xprev_ref[...], 0.0)
    xpad_ref[HALO:HALO + rows, :] = xm_ref[...]
    xpad_ref[HALO + rows:2 * HALO + rows, :] = jnp.where(i < n - 1, xnext_ref[...], 0.0)
    pad = SSD_CONV_W // 2
    acc = cb_ref[...] + cw_ref[0:1, :] * xpad_ref[HALO - pad:HALO - pad + rows, :]
    for k in range(1, SSD_CONV_W):
        acc = acc + cw_ref[k:k + 1, :] * xpad_ref[HALO - pad + k:HALO - pad + k + rows, :]
    xc_ref[...] = _silu(acc).astype(xc_ref.dtype)

    a = -jnp.exp(alog_ref[...])
    mask = _chunk_mask(False)
    tri = tri_ref[...]
    expand = exp_ref[...]

    def body(c, carry):
        r0 = pl.multiple_of(c * SSD_CHUNK, SSD_CHUNK)
        xc = xc_ref[pl.ds(r0, SSD_CHUNK), :]
        dt = _softplus(dtr_ref[pl.ds(r0, SSD_CHUNK), :] + dtb_ref[...])
        xb = xc[:, 0:SSD_WIDTH]
        y = _ssd_chunk(xb, xc[:, SSD_WIDTH:SSD_WIDTH + SSD_BC], xc[:, SSD_WIDTH + SSD_BC:], dt, dt * a,
                       tri, mask, expand, state_ref, 0, False)
        yf_ref[pl.ds(r0, SSD_CHUNK), :] = y + xb.astype(F32) * dskip_ref[...]
        return carry

    lax.fori_loop(0, rows // SSD_CHUNK, body, 0)


def _ssd_bwd_kernel(xc_ref, dtr_ref, yf_ref, z_ref, dtb_ref, alog_ref, ng_ref, tri_ref, exp_ref,
                    o_ref, state_ref):
    i = pl.program_id(1)
    rows = xc_ref.shape[0]
    nchunk = rows // SSD_CHUNK

    @pl.when(i == 0)
    def _():
        state_ref[...] = jnp.zeros_like(state_ref)

    a = -jnp.exp(alog_ref[...])
    mask = _chunk_mask(True)
    tri = tri_ref[...]
    expand = exp_ref[...]

    def body(j, carry):
        r0 = pl.multiple_of((nchunk - 1 - j) * SSD_CHUNK, SSD_CHUNK)
        xc = xc_ref[pl.ds(r0, SSD_CHUNK), :]
        dt = _softplus(dtr_ref[pl.ds(r0, SSD_CHUNK), :] + dtb_ref[...])
        y = _ssd_chunk(xc[:, 0:SSD_WIDTH], xc[:, SSD_WIDTH:SSD_WIDTH + SSD_BC], xc[:, SSD_WIDTH + SSD_BC:],
                       dt, dt * a, tri, mask, expand, state_ref, SSD_HEADS, True)
        y = (y + yf_ref[pl.ds(r0, SSD_CHUNK), :]) * _silu(z_ref[pl.ds(r0, SSD_CHUNK), :])
        outs = []
        for g in range(SSD_GROUPS):
            yg = y[:, g * GROUP_WIDTH:(g + 1) * GROUP_WIDTH]
            outs.append(yg * lax.rsqrt(jnp.mean(yg * yg, axis=-1, keepdims=True) + NORM_EPS))
        o_ref[pl.ds(r0, SSD_CHUNK), :] = (jnp.concatenate(outs, axis=1) * ng_ref[...]).astype(o_ref.dtype)
        return carry

    lax.fori_loop(0, nchunk, body, 0)


def _ssd_constants():
    r = np.arange(SSD_CHUNK)
    tri_f = (r[None, :] <= r[:, None]).astype(np.float32)
    tri_b = (r[None, :] >= r[:, None]).astype(np.float32)
    ch = np.arange(SSD_WIDTH) // SSD_HEAD_DIM
    lanes = np.arange(LANES)
    exp_f = (lanes[:, None] == ch[None, :]).astype(np.float32)
    exp_b = (lanes[:, None] == ch[None, :] + SSD_HEADS).astype(np.float32)
    return tuple(jnp.asarray(m, BF16) for m in (tri_f, tri_b, exp_f, exp_b))


def ssd_mixer(z, xbc, dtr, conv_w, conv_b, dt_bias, a_log, d_skip, norm_g, batch, seq, rows=512):
    t = xbc.shape[0]
    rows = min(rows, seq)
    n = seq // rows
    hb = rows // HALO
    tri_f, tri_b, exp_f, exp_b = _ssd_constants()
    lane_pad = lambda v: jnp.pad(v.reshape(1, -1), ((0, 0), (0, LANES - v.size)))
    dtb = lane_pad(dt_bias)
    alog = lane_pad(a_log)
    cw = jnp.pad(conv_w, ((0, SUBLANES - SSD_CONV_W), (0, 0)))
    dskip = jnp.repeat(d_skip, SSD_HEAD_DIM).reshape(1, SSD_WIDTH)
    const = lambda shape: pl.BlockSpec(shape, lambda b, i: (0,) * len(shape))

    def tile(width, rev=False):
        if rev:
            return pl.BlockSpec((rows, width), lambda b, i: (b * n + n - 1 - i, 0))
        return pl.BlockSpec((rows, width), lambda b, i: (b * n + i, 0))

    xc, yf = pl.pallas_call(
        _ssd_fwd_kernel,
        grid=(batch, n),
        in_specs=[
            tile(SSD_CONV_DIM),
            pl.BlockSpec((HALO, SSD_CONV_DIM), lambda b, i: (jnp.maximum((b * n + i) * hb - 1, 0), 0)),
            pl.BlockSpec((HALO, SSD_CONV_DIM), lambda b, i: (jnp.minimum((b * n + i + 1) * hb, t // HALO - 1), 0)),
            tile(LANES),
            const((SUBLANES, SSD_CONV_DIM)), const((1, SSD_CONV_DIM)), const((1, LANES)), const((1, LANES)),
            const((1, SSD_WIDTH)), const((SSD_CHUNK, SSD_CHUNK)), const((LANES, SSD_WIDTH)),
        ],
        out_specs=[tile(SSD_CONV_DIM), tile(SSD_WIDTH)],
        out_shape=[jax.ShapeDtypeStruct((t, SSD_CONV_DIM), BF16), jax.ShapeDtypeStruct((t, SSD_WIDTH), F32)],
        scratch_shapes=[pltpu.VMEM((rows + 2 * HALO, SSD_CONV_DIM), F32),
                        pltpu.VMEM((SSD_GROUPS, SSD_STATE, GROUP_WIDTH), F32)],
        compiler_params=_cparams(("parallel", "arbitrary")),
        name="ssd_forward",
    )(xbc, xbc, xbc, dtr, cw, conv_b.reshape(1, -1), dtb, alog, dskip, tri_f, exp_f)

    return pl.pallas_call(
        _ssd_bwd_kernel,
        grid=(batch, n),
        in_specs=[
            tile(SSD_CONV_DIM, True), tile(LANES, True), tile(SSD_WIDTH, True), tile(SSD_WIDTH, True),
            const((1, LANES)), const((1, LANES)), const((1, SSD_WIDTH)),
            const((SSD_CHUNK, SSD_CHUNK)), const((LANES, SSD_WIDTH)),
        ],
        out_specs=tile(SSD_WIDTH, True),
        out_shape=jax.ShapeDtypeStruct((t, SSD_WIDTH), BF16),
        scratch_shapes=[pltpu.VMEM((SSD_GROUPS, SSD_STATE, GROUP_WIDTH), F32)],
        compiler_params=_cparams(("parallel", "arbitrary")),
        name="ssd_backward",
    )(xc, dtr, yf, z, dtb, alog, norm_g.reshape(1, -1), tri_b, exp_b)


def _rope_tables(seq):
    pos = np.arange(seq)
    row_idx = (pos // GRID_W).astype(np.float32)
    col_idx = (pos % GRID_W).astype(np.float32)
    axis_dim = GQA_HEAD_DIM // 2
    inv_freq = jnp.asarray(ROPE_THETA, F32) ** (-jnp.arange(0, axis_dim, 2, dtype=F32) / axis_dim)
    ang_r = jnp.asarray(row_idx)[:, None] * inv_freq
    ang_c = jnp.asarray(col_idx)[:, None] * inv_freq
    cr, sr, cc, sc = jnp.cos(ang_r), jnp.sin(ang_r), jnp.cos(ang_c), jnp.sin(ang_c)
    cos_tab = jnp.concatenate([cr, cr, cc, cc], axis=-1)
    sin_tab = jnp.concatenate([-sr, sr, -sc, sc], axis=-1)
    return cos_tab, sin_tab


def _relayout_w_in(w_in):
    ref_sizes = [("z", SSD_WIDTH), ("xbc", SSD_CONV_DIM), ("dt", 2 * SSD_HEADS), ("gq", GQA_WIDTH),
                 ("gk", GQA_KV_WIDTH), ("gv", GQA_KV_WIDTH), ("dq", DIFF_QK_WIDTH),
                 ("dk", DIFF_QK_WIDTH), ("dv", DIFF_WIDTH)]
    parts, o = {}, 0
    for k, n in ref_sizes:
        parts[k] = w_in[:, :, o:o + n]
        o += n
    parts["dt"] = jnp.pad(parts["dt"], ((0, 0), (0, 0), (0, DT_PAD - 2 * SSD_HEADS)))
    return jnp.concatenate([parts[k] for k in SEG_SIZES], axis=-1).astype(BF16)


def _encoder_group(x, mod_tab, row0, weights, final_g):
    b, s, d = x.shape
    depth = weights["w_in"].shape[0]
    x2 = x.reshape(b * s, d)
    cos_tab, sin_tab = _rope_tables(s)
    for li in range(depth):
        w = {k: v[li] for k, v in weights.items()}
        z, xbc, dtr, gq, gk, gv, dq, dk, dv = in_projection(
            x2, mod_tab, li, row0, s, w["norm1_g"], w["w_in"], w["q_norm_g"], w["k_norm_g"],
            cos_tab, sin_tab)

        y_ssd = ssd_mixer(z, xbc, dtr, w["conv_w"], w["conv_b"], w["dt_bias"], w["a_log"], w["d_skip"],
                          w["ssd_norm_g"], b, s)
        y_gqa = gqa_attention(gq, gk, gv, b, s)
        lam_init = 0.8 - 0.6 * math.exp(-0.3 * li)
        y_diff = diff_attention(dq, dk, dv, w["diff_lambda"], w["diff_subln_g"], lam_init, b, s)

        x2 = out_projection(x2, y_ssd, y_gqa, y_diff, w["w_out"], mod_tab, li, row0, s)
        x2 = ffn(x2, mod_tab, li, row0, s, w["norm2_g"], w["w_gate"], w["w_up"], w["w_down"])
    return final_norm(x2, final_g).reshape(b, s, d)


def kernel(x_prompt, x_sample, c_prompt, c_sample, w_mod, b_mod, norm1_g, w_in, conv_w, conv_b, dt_bias,
           a_log, d_skip, ssd_norm_g, q_norm_g, k_norm_g, diff_lambda, diff_subln_g, w_out, norm2_g,
           w_gate, w_up, w_down, final_g):
    depth, d, _ = w_mod.shape
    bp, bs = c_prompt.shape[0], c_sample.shape[0]
    assert bp + bs <= MOD_ROWS
    c_all = jnp.concatenate([c_prompt, c_sample, jnp.zeros((MOD_ROWS - bp - bs, d), F32)], axis=0)
    mod = modulation(c_all, w_mod, b_mod)
    mod_tab = mod.reshape(depth * MOD_ROWS * 6, 1, d)

    weights = dict(
        norm1_g=norm1_g, w_in=_relayout_w_in(w_in), conv_w=conv_w, conv_b=conv_b, dt_bias=dt_bias,
        a_log=a_log, d_skip=d_skip, ssd_norm_g=ssd_norm_g, q_norm_g=q_norm_g, k_norm_g=k_norm_g,
        diff_lambda=diff_lambda, diff_subln_g=diff_subln_g, w_out=w_out.astype(BF16), norm2_g=norm2_g,
        w_gate=w_gate.astype(BF16), w_up=w_up.astype(BF16), w_down=w_down.astype(BF16))
    y_prompt = _encoder_group(x_prompt, mod_tab, 0, weights, final_g)
    y_sample = _encoder_group(x_sample, mod_tab, bp, weights, final_g)
    return (y_prompt, y_sample)
```

```python
import functools
import math

import jax
import jax.numpy as jnp
import numpy as np
from jax import lax
from jax.experimental import pallas as pl
from jax.experimental.pallas import tpu as pltpu

F32 = jnp.float32
BF16 = jnp.bfloat16

NORM_EPS = 1e-6
GRID_W = 64
ROPE_THETA = 10000.0
Q_BLOCK = 128

SSD_HEADS = 16
SSD_HEAD_DIM = 64
SSD_WIDTH = SSD_HEADS * SSD_HEAD_DIM
SSD_GROUPS = 2
SSD_STATE = 128
SSD_CONV_W = 5
SSD_CHUNK = 128
SSD_BC = SSD_GROUPS * SSD_STATE
SSD_CONV_DIM = SSD_WIDTH + 2 * SSD_BC

GQA_HEADS = 4
GQA_KV_HEADS = 2
GQA_HEAD_DIM = 128
GQA_WIDTH = GQA_HEADS * GQA_HEAD_DIM
GQA_KV_WIDTH = GQA_KV_HEADS * GQA_HEAD_DIM

DIFF_HEADS = 4
DIFF_QK_DIM = 64
DIFF_V_DIM = 128
DIFF_QK_WIDTH = DIFF_HEADS * 2 * DIFF_QK_DIM
DIFF_WIDTH = DIFF_HEADS * DIFF_V_DIM

MIX_WIDTH = SSD_WIDTH + GQA_WIDTH + DIFF_WIDTH

LANES = 128
SUBLANES = 8
VMEM_LIMIT_BYTES = 58 * 1024 * 1024

DT_PAD = LANES
SEG_SIZES = dict(z=SSD_WIDTH, xbc=SSD_CONV_DIM, gq=GQA_WIDTH, gk=GQA_KV_WIDTH, gv=GQA_KV_WIDTH,
                 dq=DIFF_QK_WIDTH, dk=DIFF_QK_WIDTH, dv=DIFF_WIDTH, dt=DT_PAD)
SEG_OFF = {}
_o = 0
for _k, _v in SEG_SIZES.items():
    SEG_OFF[_k] = _o
    _o += _v
IN_COLS_PAD = _o

MOD_ROWS = 16

LOG2E = 1.4426950408889634
GQA_Q_SCALE = LOG2E * GQA_HEAD_DIM ** -0.5
DIFF_Q_SCALE = LOG2E * DIFF_QK_DIM ** -0.5


def _cparams(sem):
    return pltpu.CompilerParams(dimension_semantics=sem, vmem_limit_bytes=VMEM_LIMIT_BYTES)


def _silu(x):
    return x * (1.0 / (1.0 + jnp.exp(-x)))


def _mod_kernel(c_ref, w_ref, b_ref, o_ref):
    c = _silu(c_ref[...]).astype(BF16)
    w = w_ref[0].astype(BF16)
    o_ref[0] = jnp.dot(c, w, preferred_element_type=F32) + b_ref[0]


def modulation(c_all, w_mod, b_mod, tn=1024):
    depth, d, n = w_mod.shape
    return pl.pallas_call(
        _mod_kernel,
        grid=(depth, n // tn),
        in_specs=[
            pl.BlockSpec((MOD_ROWS, d), lambda l, j: (0, 0)),
            pl.BlockSpec((1, d, tn), lambda l, j: (l, 0, j)),
            pl.BlockSpec((1, 1, tn), lambda l, j: (l, 0, j)),
        ],
        out_specs=pl.BlockSpec((1, MOD_ROWS, tn), lambda l, j: (l, 0, j)),
        out_shape=jax.ShapeDtypeStruct((depth, MOD_ROWS, n), F32),
        compiler_params=_cparams(("arbitrary", "arbitrary")),
        name="modulation",
    )(c_all, w_mod, b_mod.reshape(depth, 1, n))


def _mod_spec(d, layer, which, row0, tiles_per_seq):
    def imap(i, *_):
        return ((layer * MOD_ROWS + row0 + i // tiles_per_seq) * 6 + which, 0, 0)
    return pl.BlockSpec((1, 1, d), imap)


def _swap32(t):
    lane = lax.broadcasted_iota(jnp.int32, t.shape, 1)
    return jnp.where((lane % 64) < 32, pltpu.roll(t, 96, 1), pltpu.roll(t, 32, 1))


def _inproj_kernel(x_ref, sc_ref, sh_ref, g_ref, w_ref, qg_ref, kg_ref, cos_ref, sin_ref,
                   z_ref, xbc_ref, dt_ref, gq_ref, gk_ref, gv_ref, dq_ref, dk_ref, dv_ref):
    x = x_ref[...]
    inv = lax.rsqrt(jnp.mean(x * x, axis=-1, keepdims=True) + NORM_EPS)
    h = (x * inv * g_ref[...]) * (1.0 + sc_ref[0]) + sh_ref[0]
    hb = h.astype(BF16)

    def proj(name, lo=0, n=None):
        o = SEG_OFF[name] + lo
        n = SEG_SIZES[name] if n is None else n
        return jnp.dot(hb, w_ref[:, o:o + n], preferred_element_type=F32)

    z_ref[...] = proj("z")
    xbc_ref[...] = proj("xbc")
    dt_ref[...] = proj("dt")
    gv_ref[...] = proj("gv").astype(BF16)
    dq_ref[...] = (proj("dq") * DIFF_Q_SCALE).astype(BF16)
    dk_ref[...] = proj("dk").astype(BF16)
    dv_ref[...] = proj("dv").astype(BF16)

    cos = cos_ref[...]
    sin = sin_ref[...]

    def norm_rope(t, gain):
        t = t * lax.rsqrt(jnp.mean(t * t, axis=-1, keepdims=True) + NORM_EPS) * gain
        return t * cos + _swap32(t) * sin

    for hd in range(GQA_HEADS):
        t = proj("gq", hd * GQA_HEAD_DIM, GQA_HEAD_DIM)
        gq_ref[:, hd * GQA_HEAD_DIM:(hd + 1) * GQA_HEAD_DIM] = (norm_rope(t, qg_ref[...]) * GQA_Q_SCALE).astype(BF16)
    for hd in range(GQA_KV_HEADS):
        t = proj("gk", hd * GQA_HEAD_DIM, GQA_HEAD_DIM)
        gk_ref[:, hd * GQA_HEAD_DIM:(hd + 1) * GQA_HEAD_DIM] = norm_rope(t, kg_ref[...]).astype(BF16)


def in_projection(x2, mod_tab, layer, row0, seq, norm_g, w_in_l, q_g, k_g, cos_tab, sin_tab, tm=512):
    t, d = x2.shape
    tm = min(tm, seq)
    tps = seq // tm
    row = lambda n: pl.BlockSpec((tm, n), lambda i: (i, 0))
    const = lambda shape: pl.BlockSpec(shape, lambda i: (0,) * len(shape))
    pos = pl.BlockSpec((tm, GQA_HEAD_DIM), lambda i: (i % tps, 0))
    outs = [("z", F32), ("xbc", F32), ("dt", F32), ("gq", BF16), ("gk", BF16), ("gv", BF16),
            ("dq", BF16), ("dk", BF16), ("dv", BF16)]
    return pl.pallas_call(
        _inproj_kernel,
        grid=(t // tm,),
        in_specs=[
            row(d),
            _mod_spec(d, layer, 1, row0, tps),
            _mod_spec(d, layer, 0, row0, tps),
            const((1, d)),
            pl.BlockSpec((d, IN_COLS_PAD), lambda i: (0, 0), pipeline_mode=pl.Buffered(1)),
            const((1, GQA_HEAD_DIM)),
            const((1, GQA_HEAD_DIM)),
            pos, pos,
        ],
        out_specs=[row(SEG_SIZES[k]) for k, _ in outs],
        out_shape=[jax.ShapeDtypeStruct((t, SEG_SIZES[k]), dt) for k, dt in outs],
        compiler_params=_cparams(("parallel",)),
        name="in_projection",
    )(x2, mod_tab, mod_tab, norm_g.reshape(1, d), w_in_l, q_g.reshape(1, -1), k_g.reshape(1, -1),
      cos_tab, sin_tab)


def _outproj_kernel(x_ref, ys_ref, yg_ref, yd_ref, w_ref, g_ref, o_ref):
    acc = jnp.dot(ys_ref[...], w_ref[0:SSD_WIDTH, :], preferred_element_type=F32)
    acc += jnp.dot(yg_ref[...], w_ref[SSD_WIDTH:SSD_WIDTH + GQA_WIDTH, :], preferred_element_type=F32)
    acc += jnp.dot(yd_ref[...], w_ref[SSD_WIDTH + GQA_WIDTH:, :], preferred_element_type=F32)
    o_ref[...] = x_ref[...] + g_ref[0] * acc


def out_projection(x2, y_ssd, y_gqa, y_diff, w_out_l, mod_tab, layer, row0, seq, tm=512):
    t, d = x2.shape
    tm = min(tm, seq)
    tps = seq // tm
    row = lambda n: pl.BlockSpec((tm, n), lambda i: (i, 0))
    return pl.pallas_call(
        _outproj_kernel,
        grid=(t // tm,),
        in_specs=[
            row(d), row(SSD_WIDTH), row(GQA_WIDTH), row(DIFF_WIDTH),
            pl.BlockSpec((MIX_WIDTH, d), lambda i: (0, 0), pipeline_mode=pl.Buffered(1)),
            _mod_spec(d, layer, 2, row0, tps),
        ],
        out_specs=row(d),
        out_shape=jax.ShapeDtypeStruct((t, d), F32),
        compiler_params=_cparams(("parallel",)),
        name="out_projection",
    )(x2, y_ssd, y_gqa, y_diff, w_out_l, mod_tab)


def _ffn_kernel(x_ref, sc_ref, sh_ref, gate_ref, g_ref, wg_ref, wu_ref, wd_ref, o_ref, h_ref, acc_ref):
    f = pl.program_id(1)

    @pl.when(f == 0)
    def _():
        x = x_ref[...]
        inv = lax.rsqrt(jnp.mean(x * x, axis=-1, keepdims=True) + NORM_EPS)
        h = (x * inv * g_ref[...]) * (1.0 + sc_ref[0]) + sh_ref[0]
        h_ref[...] = h.astype(BF16)
        acc_ref[...] = jnp.zeros_like(acc_ref)

    hb = h_ref[...]
    a = jnp.dot(hb, wg_ref[...], preferred_element_type=F32)
    b = jnp.dot(hb, wu_ref[...], preferred_element_type=F32)
    act = (_silu(a) * b).astype(BF16)
    acc_ref[...] += jnp.dot(act, wd_ref[...], preferred_element_type=F32)

    @pl.when(f == pl.num_programs(1) - 1)
    def _():
        o_ref[...] = x_ref[...] + gate_ref[0] * acc_ref[...]


def ffn(x2, mod_tab, layer, row0, seq, norm_g, wg, wu, wd, tm=512, tf=512):
    t, d = x2.shape
    dff = wg.shape[1]
    tm = min(tm, seq)
    tps = seq // tm
    return pl.pallas_call(
        _ffn_kernel,
        grid=(t // tm, dff // tf),
        in_specs=[
            pl.BlockSpec((tm, d), lambda i, f: (i, 0)),
            _mod_spec(d, layer, 4, row0, tps),
            _mod_spec(d, layer, 3, row0, tps),
            _mod_spec(d, layer, 5, row0, tps),
            pl.BlockSpec((1, d), lambda i, f: (0, 0)),
            pl.BlockSpec((d, tf), lambda i, f: (0, f)),
            pl.BlockSpec((d, tf), lambda i, f: (0, f)),
            pl.BlockSpec((tf, d), lambda i, f: (f, 0)),
        ],
        out_specs=pl.BlockSpec((tm, d), lambda i, f: (i, 0)),
        out_shape=jax.ShapeDtypeStruct((t, d), F32),
        scratch_shapes=[pltpu.VMEM((tm, d), BF16), pltpu.VMEM((tm, d), F32)],
        compiler_params=_cparams(("parallel", "arbitrary")),
        name="ffn",
    )(x2, mod_tab, mod_tab, mod_tab, norm_g.reshape(1, d), wg, wu, wd)


def _final_norm_kernel(x_ref, g_ref, o_ref):
    x = x_ref[...]
    o_ref[...] = x * lax.rsqrt(jnp.mean(x * x, axis=-1, keepdims=True) + NORM_EPS) * g_ref[...]


def final_norm(x2, g, tm=512):
    t, d = x2.shape
    tm = math.gcd(tm, t)
    return pl.pallas_call(
        _final_norm_kernel,
        grid=(t // tm,),
        in_specs=[pl.BlockSpec((tm, d), lambda i: (i, 0)), pl.BlockSpec((1, d), lambda i: (0, 0))],
        out_specs=pl.BlockSpec((tm, d), lambda i: (i, 0)),
        out_shape=jax.ShapeDtypeStruct((t, d), F32),
        compiler_params=_cparams(("parallel",)),
        name="final_norm",
    )(x2, g.reshape(1, d))


ATTN_CHUNK = 512


def _qk(q, k):
    return lax.dot_general(q, k, (((1,), (1,)), ((), ())), preferred_element_type=F32)


def _attend_lagged(q, k_ref, v_ref, s_ref, m_ref, bias_fn=None):
    m_rows = q.shape[0]
    seq = k_ref.shape[0]
    ck = min(ATTN_CHUNK, seq)
    m_prev = jnp.max(m_ref[...], axis=-1, keepdims=True)
    mx = jnp.full((m_rows, LANES), -jnp.inf, F32)
    ls = jnp.zeros((m_rows, LANES), F32)
    acc = jnp.zeros((m_rows, v_ref.shape[1]), F32)
    for c0 in range(0, seq, ck):
        p = jnp.exp2(s_ref[:, c0:c0 + ck] - m_prev)
        for t in range(0, ck, LANES):
            ls = ls + p[:, t:t + LANES]
        acc = acc + jnp.dot(p.astype(BF16), v_ref[c0:c0 + ck, :], preferred_element_type=F32)
        s = _qk(q, k_ref[c0:c0 + ck, :])
        if bias_fn is not None:
            s = bias_fn(s, c0)
        s_ref[:, c0:c0 + ck] = s
        for t in range(0, ck, LANES):
            mx = jnp.maximum(mx, s[:, t:t + LANES])
    m_ref[...] = mx
    return acc * (1.0 / jnp.sum(ls, axis=-1, keepdims=True))


def _init_lag_state(s_ref, m_ref):
    @pl.when(pl.program_id(0) == 0)
    def _():
        s_ref[...] = jnp.zeros_like(s_ref)
        m_ref[...] = jnp.zeros_like(m_ref)


def _gqa_kernel(q_ref, k_ref, v_ref, o_ref, s_ref, m_ref):
    tq = q_ref.shape[0]
    d = GQA_HEAD_DIM
    _init_lag_state(s_ref, m_ref)
    q = jnp.concatenate([q_ref[:, 0:d], q_ref[:, d:2 * d]], axis=0)
    o = _attend_lagged(q, k_ref, v_ref, s_ref, m_ref)
    o_ref[:, 0:d] = o[0:tq].astype(o_ref.dtype)
    o_ref[:, d:2 * d] = o[tq:2 * tq].astype(o_ref.dtype)


ATTN_TQ = 256


def _attn_tile(seq):
    return min(seq, ATTN_TQ)


def _lag_specs(n_heads, nq, n):
    def decode(t):
        bh = t // nq
        return bh // n_heads, bh % n_heads, t % nq

    def tile_map(lag):
        def imap(t):
            b, h, i = decode(jnp.clip(t - lag, 0, n - 1))
            return (b * nq + i, h)
        return imap

    def kv_map(lag):
        def imap(t):
            b, h, _ = decode(jnp.clip(t - lag, 0, n - 1))
            return (b, h)
        return imap

    return tile_map, kv_map


def gqa_attention(gq, gk, gv, batch, seq):
    t = gq.shape[0]
    tq = _attn_tile(seq)
    nq = seq // tq
    rep = GQA_HEADS // GQA_KV_HEADS
    w = rep * GQA_HEAD_DIM
    n = batch * GQA_KV_HEADS * nq
    tile_map, kv_map = _lag_specs(GQA_KV_HEADS, nq, n)
    return pl.pallas_call(
        _gqa_kernel,
        grid=(n + 1,),
        in_specs=[
            pl.BlockSpec((tq, w), tile_map(0)),
            pl.BlockSpec((seq, GQA_HEAD_DIM), kv_map(0)),
            pl.BlockSpec((seq, GQA_HEAD_DIM), kv_map(1)),
        ],
        out_specs=pl.BlockSpec((tq, w), tile_map(1)),
        out_shape=jax.ShapeDtypeStruct((t, GQA_WIDTH), BF16),
        scratch_shapes=[pltpu.VMEM((rep * tq, seq), F32), pltpu.VMEM((rep * tq, LANES), F32)],
        compiler_params=_cparams(("arbitrary",)),
        name="gqa_attention",
    )(gq, gk, gv)


def _diff_kernel(lam_init, nq, n, q_ref, k_ref, v_ref, lp_ref, g_ref, o_ref, s_ref, m_ref):
    tq = q_ref.shape[0]
    ck = min(ATTN_CHUNK, k_ref.shape[0])
    cur = jnp.minimum(pl.program_id(0), n - 1)
    hd = (cur // nq) % DIFF_HEADS
    q0 = (cur % nq) * tq
    _init_lag_state(s_ref, m_ref)
    q = q_ref[...]
    lane = lax.broadcasted_iota(jnp.int32, q.shape, 1)
    zero = jnp.zeros_like(q)
    q12 = jnp.concatenate([jnp.where(lane < DIFF_QK_DIM, q, zero), jnp.where(lane >= DIFF_QK_DIM, q, zero)], axis=0)
    slope = jnp.exp2(-8.0 * (jnp.full((1, 1), hd, jnp.int32).astype(F32) + 1.0) / DIFF_HEADS)
    neg_slope = -LOG2E * slope
    rel0 = (lax.broadcasted_iota(jnp.int32, (tq, ck), 1) - lax.broadcasted_iota(jnp.int32, (tq, ck), 0)).astype(F32)

    def alibi(s, c0):
        off = jnp.full((1, 1), c0 - q0, jnp.int32).astype(F32)
        bias = jnp.abs(rel0 + off) * neg_slope
        return jnp.concatenate([s[0:tq] + bias, s[tq:2 * tq] + bias], axis=0)

    o = _attend_lagged(q12, k_ref, v_ref, s_ref, m_ref, alibi)
    lp = lp_ref[...]
    lam = (jnp.exp(jnp.sum(lp[0:1] * lp[1:2], axis=-1, keepdims=True))
           - jnp.exp(jnp.sum(lp[2:3] * lp[3:4], axis=-1, keepdims=True)) + lam_init)
    o = o[0:tq] - lam * o[tq:2 * tq]
    o = o * lax.rsqrt(jnp.mean(o * o, axis=-1, keepdims=True) + NORM_EPS) * g_ref[...]
    o_ref[...] = (o * (1.0 - lam_init)).astype(o_ref.dtype)


def diff_attention(dq, dk, dv, diff_lambda, subln_g, lam_init, batch, seq):
    t = dq.shape[0]
    tq = _attn_tile(seq)
    nq = seq // tq
    w = DIFF_V_DIM
    n = batch * DIFF_HEADS * nq
    tile_map, kv_map = _lag_specs(DIFF_HEADS, nq, n)
    return pl.pallas_call(
        functools.partial(_diff_kernel, lam_init, nq, n),
        grid=(n + 1,),
        in_specs=[
            pl.BlockSpec((tq, w), tile_map(0)),
            pl.BlockSpec((seq, w), kv_map(0)),
            pl.BlockSpec((seq, w), kv_map(1)),
            pl.BlockSpec((4, DIFF_QK_DIM), lambda t: (0, 0)),
            pl.BlockSpec((1, w), lambda t: (0, 0)),
        ],
        out_specs=pl.BlockSpec((tq, w), tile_map(1)),
        out_shape=jax.ShapeDtypeStruct((t, DIFF_WIDTH), BF16),
        scratch_shapes=[pltpu.VMEM((2 * tq, seq), F32), pltpu.VMEM((2 * tq, LANES), F32)],
        compiler_params=_cparams(("arbitrary",)),
        name="diff_attention",
    )(dq, dk, dv, diff_lambda, subln_g.reshape(1, w))


HALO = SUBLANES
HEADS_PER_GROUP = SSD_HEADS // SSD_GROUPS
GROUP_WIDTH = SSD_WIDTH // SSD_GROUPS


def _split3(x):
    hi = x.astype(BF16)
    r1 = x - hi.astype(F32)
    mid = r1.astype(BF16)
    lo = (r1 - mid.astype(F32)).astype(BF16)
    return hi, mid, lo


def _dot_sel_rhs(x, sel):
    hi, mid, lo = _split3(x)
    return (jnp.dot(hi, sel, preferred_element_type=F32) + jnp.dot(mid, sel, preferred_element_type=F32)
            + jnp.dot(lo, sel, preferred_element_type=F32))


def _dot_sel_lhs(sel, x):
    hi, mid, lo = _split3(x)
    return (jnp.dot(sel, hi, preferred_element_type=F32) + jnp.dot(sel, mid, preferred_element_type=F32)
            + jnp.dot(sel, lo, preferred_element_type=F32))


def _softplus(x):
    return jnp.maximum(x, 0.0) + jnp.log1p(jnp.exp(-jnp.abs(x)))


def _ssd_chunk(xb, bm, cm, dt, da, tri, mask, expand, state_ref, lane0, reverse):
    L = SSD_CHUNK
    acs = _dot_sel_lhs(tri, da)
    acs_t = acs.T
    dt_t = dt.T
    tot = acs[0:1, :] if reverse else acs[L - 1:L, :]
    w = dt * jnp.exp(tot - acs)
    din = jnp.exp(acs)
    wx = _dot_sel_rhs(jnp.concatenate([w, din], axis=0), expand)
    w_x, din_x = wx[0:L], wx[L:2 * L]
    cd_x = din_x[0:1] if reverse else din_x[L - 1:L]
    xw = (xb.astype(F32) * w_x).astype(BF16)
    lane = lax.broadcasted_iota(jnp.int32, (L, 2 * SSD_HEAD_DIM), 1)
    ys = []
    for g in range(SSD_GROUPS):
        bg = bm[:, g * SSD_STATE:(g + 1) * SSD_STATE]
        cg = cm[:, g * SSD_STATE:(g + 1) * SSD_STATE]
        scores = _qk(cg, bg)
        prev = state_ref[g]
        y_off = jnp.dot(cg, prev.astype(BF16), preferred_element_type=F32)
        cols = slice(g * GROUP_WIDTH, (g + 1) * GROUP_WIDTH)
        y_g = y_off * din_x[:, cols]
        pieces = []
        for pr in range(HEADS_PER_GROUP // 2):
            c0 = g * GROUP_WIDTH + pr * 2 * SSD_HEAD_DIM
            xp = xb[:, c0:c0 + 2 * SSD_HEAD_DIM]
            acc = None
            for half in range(2):
                row = lane0 + g * HEADS_PER_GROUP + 2 * pr + half
                diff = acs[:, row:row + 1] - acs_t[row:row + 1, :]
                m = scores * jnp.exp(jnp.where(mask, diff, -jnp.inf)) * dt_t[row:row + 1, :]
                keep = (lane < SSD_HEAD_DIM) if half == 0 else (lane >= SSD_HEAD_DIM)
                xh = jnp.where(keep, xp, jnp.zeros_like(xp))
                part = jnp.dot(m.astype(BF16), xh, preferred_element_type=F32)
                acc = part if acc is None else acc + part
            pieces.append(acc)
        ys.append(y_g + jnp.concatenate(pieces, axis=1))
        new = lax.dot_general(bg, xw[:, cols], (((0,), (0,)), ((), ())), preferred_element_type=F32)
        state_ref[g] = prev * cd_x[:, cols] + new
    return jnp.concatenate(ys, axis=1)


def _chunk_mask(reverse):
    r = lax.broadcasted_iota(jnp.int32, (SSD_CHUNK, SSD_CHUNK), 0)
    c = lax.broadcasted_iota(jnp.int32, (SSD_CHUNK, SSD_CHUNK), 1)
    return (c >= r) if reverse else (r >= c)


def _ssd_fwd_kernel(xm_ref, xprev_ref, xnext_ref, dtr_ref, cw_ref, cb_ref, dtb_ref, alog_ref, dskip_ref,
                    tri_ref, exp_ref, xc_ref, yf_ref, xpad_ref, state_ref):
    i = pl.program_id(1)
    n = pl.num_programs(1)
    rows = xm_ref.shape[0]

    @pl.when(i == 0)
    def _():
        state_ref[...] = jnp.zeros_like(state_ref)

    xpad_ref[0:HALO, :] = jnp.where(i > 0, xprev_ref[...], 0.0)
    xpad_ref[HALO:HALO + rows, :] = xm_ref[...]
    xpad_ref[HALO + rows:2 * HALO + rows, :] = jnp.where(i < n - 1, xnext_ref[...], 0.0)
    pad = SSD_CONV_W // 2
    acc = cb_ref[...] + cw_ref[0:1, :] * xpad_ref[HALO - pad:HALO - pad + rows, :]
    for k in range(1, SSD_CONV_W):
        acc = acc + cw_ref[k:k + 1, :] * xpad_ref[HALO - pad + k:HALO - pad + k + rows, :]
    xc_ref[...] = _silu(acc).astype(xc_ref.dtype)

    a = -jnp.exp(alog_ref[...])
    mask = _chunk_mask(False)
    tri = tri_ref[...]
    expand = exp_ref[...]

    def body(c, carry):
        r0 = pl.multiple_of(c * SSD_CHUNK, SSD_CHUNK)
        xc = xc_ref[pl.ds(r0, SSD_CHUNK), :]
        dt = _softplus(dtr_ref[pl.ds(r0, SSD_CHUNK), :] + dtb_ref[...])
        xb = xc[:, 0:SSD_WIDTH]
        y = _ssd_chunk(xb, xc[:, SSD_WIDTH:SSD_WIDTH + SSD_BC], xc[:, SSD_WIDTH + SSD_BC:], dt, dt * a,
                       tri, mask, expand, state_ref, 0, False)
        yf_ref[pl.ds(r0, SSD_CHUNK), :] = y + xb.astype(F32) * dskip_ref[...]
        return carry

    lax.fori_loop(0, rows // SSD_CHUNK, body, 0)


def _ssd_bwd_kernel(xc_ref, dtr_ref, yf_ref, z_ref, dtb_ref, alog_ref, ng_ref, tri_ref, exp_ref,
                    o_ref, state_ref):
    i = pl.program_id(1)
    rows = xc_ref.shape[0]
    nchunk = rows // SSD_CHUNK

    @pl.when(i == 0)
    def _():
        state_ref[...] = jnp.zeros_like(state_ref)

    a = -jnp.exp(alog_ref[...])
    mask = _chunk_mask(True)
    tri = tri_ref[...]
    expand = exp_ref[...]

    def body(j, carry):
        r0 = pl.multiple_of((nchunk - 1 - j) * SSD_CHUNK, SSD_CHUNK)
        xc = xc_ref[pl.ds(r0, SSD_CHUNK), :]
        dt = _softplus(dtr_ref[pl.ds(r0, SSD_CHUNK), :] + dtb_ref[...])
        y = _ssd_chunk(xc[:, 0:SSD_WIDTH], xc[:, SSD_WIDTH:SSD_WIDTH + SSD_BC], xc[:, SSD_WIDTH + SSD_BC:],
                       dt, dt * a, tri, mask, expand, state_ref, SSD_HEADS, True)
        y = (y + yf_ref[pl.ds(r0, SSD_CHUNK), :]) * _silu(z_ref[pl.ds(r0, SSD_CHUNK), :])
        outs = []
        for g in range(SSD_GROUPS):
            yg = y[:, g * GROUP_WIDTH:(g + 1) * GROUP_WIDTH]
            outs.append(yg * lax.rsqrt(jnp.mean(yg * yg, axis=-1, keepdims=True) + NORM_EPS))
        o_ref[pl.ds(r0, SSD_CHUNK), :] = (jnp.concatenate(outs, axis=1) * ng_ref[...]).astype(o_ref.dtype)
        return carry

    lax.fori_loop(0, nchunk, body, 0)


def _ssd_constants():
    r = np.arange(SSD_CHUNK)
    tri_f = (r[None, :] <= r[:, None]).astype(np.float32)
    tri_b = (r[None, :] >= r[:, None]).astype(np.float32)
    ch = np.arange(SSD_WIDTH) // SSD_HEAD_DIM
    lanes = np.arange(LANES)
    exp_f = (lanes[:, None] == ch[None, :]).astype(np.float32)
    exp_b = (lanes[:, None] == ch[None, :] + SSD_HEADS).astype(np.float32)
    return tuple(jnp.asarray(m, BF16) for m in (tri_f, tri_b, exp_f, exp_b))


def ssd_mixer(z, xbc, dtr, conv_w, conv_b, dt_bias, a_log, d_skip, norm_g, batch, seq, rows=512):
    t = xbc.shape[0]
    rows = min(rows, seq)
    n = seq // rows
    hb = rows // HALO
    tri_f, tri_b, exp_f, exp_b = _ssd_constants()
    lane_pad = lambda v: jnp.pad(v.reshape(1, -1), ((0, 0), (0, LANES - v.size)))
    dtb = lane_pad(dt_bias)
    alog = lane_pad(a_log)
    cw = jnp.pad(conv_w, ((0, SUBLANES - SSD_CONV_W), (0, 0)))
    dskip = jnp.repeat(d_skip, SSD_HEAD_DIM).reshape(1, SSD_WIDTH)
    const = lambda shape: pl.BlockSpec(shape, lambda b, i: (0,) * len(shape))

    def tile(width, rev=False):
        if rev:
            return pl.BlockSpec((rows, width), lambda b, i: (b * n + n - 1 - i, 0))
        return pl.BlockSpec((rows, width), lambda b, i: (b * n + i, 0))

    xc, yf = pl.pallas_call(
        _ssd_fwd_kernel,
        grid=(batch, n),
        in_specs=[
            tile(SSD_CONV_DIM),
            pl.BlockSpec((HALO, SSD_CONV_DIM), lambda b, i: (jnp.maximum((b * n + i) * hb - 1, 0), 0)),
            pl.BlockSpec((HALO, SSD_CONV_DIM), lambda b, i: (jnp.minimum((b * n + i + 1) * hb, t // HALO - 1), 0)),
            tile(LANES),
            const((SUBLANES, SSD_CONV_DIM)), const((1, SSD_CONV_DIM)), const((1, LANES)), const((1, LANES)),
            const((1, SSD_WIDTH)), const((SSD_CHUNK, SSD_CHUNK)), const((LANES, SSD_WIDTH)),
        ],
        out_specs=[tile(SSD_CONV_DIM), tile(SSD_WIDTH)],
        out_shape=[jax.ShapeDtypeStruct((t, SSD_CONV_DIM), BF16), jax.ShapeDtypeStruct((t, SSD_WIDTH), F32)],
        scratch_shapes=[pltpu.VMEM((rows + 2 * HALO, SSD_CONV_DIM), F32),
                        pltpu.VMEM((SSD_GROUPS, SSD_STATE, GROUP_WIDTH), F32)],
        compiler_params=_cparams(("parallel", "arbitrary")),
        name="ssd_forward",
    )(xbc, xbc, xbc, dtr, cw, conv_b.reshape(1, -1), dtb, alog, dskip, tri_f, exp_f)

    return pl.pallas_call(
        _ssd_bwd_kernel,
        grid=(batch, n),
        in_specs=[
            tile(SSD_CONV_DIM, True), tile(LANES, True), tile(SSD_WIDTH, True), tile(SSD_WIDTH, True),
            const((1, LANES)), const((1, LANES)), const((1, SSD_WIDTH)),
            const((SSD_CHUNK, SSD_CHUNK)), const((LANES, SSD_WIDTH)),
        ],
        out_specs=tile(SSD_WIDTH, True),
        out_shape=jax.ShapeDtypeStruct((t, SSD_WIDTH), BF16),
        scratch_shapes=[pltpu.VMEM((SSD_GROUPS, SSD_STATE, GROUP_WIDTH), F32)],
        compiler_params=_cparams(("parallel", "arbitrary")),
        name="ssd_backward",
    )(xc, dtr, yf, z, dtb, alog, norm_g.reshape(1, -1), tri_b, exp_b)


def _rope_tables(seq):
    pos = np.arange(seq)
    row_idx = (pos // GRID_W).astype(np.float32)
    col_idx = (pos % GRID_W).astype(np.float32)
    axis_dim = GQA_HEAD_DIM // 2
    inv_freq = jnp.asarray(ROPE_THETA, F32) ** (-jnp.arange(0, axis_dim, 2, dtype=F32) / axis_dim)
    ang_r = jnp.asarray(row_idx)[:, None] * inv_freq
    ang_c = jnp.asarray(col_idx)[:, None] * inv_freq
    cr, sr, cc, sc = jnp.cos(ang_r), jnp.sin(ang_r), jnp.cos(ang_c), jnp.sin(ang_c)
    cos_tab = jnp.concatenate([cr, cr, cc, cc], axis=-1)
    sin_tab = jnp.concatenate([-sr, sr, -sc, sc], axis=-1)
    return cos_tab, sin_tab


def _relayout_w_in(w_in):
    ref_sizes = [("z", SSD_WIDTH), ("xbc", SSD_CONV_DIM), ("dt", 2 * SSD_HEADS), ("gq", GQA_WIDTH),
                 ("gk", GQA_KV_WIDTH), ("gv", GQA_KV_WIDTH), ("dq", DIFF_QK_WIDTH),
                 ("dk", DIFF_QK_WIDTH), ("dv", DIFF_WIDTH)]
    parts, o = {}, 0
    for k, n in ref_sizes:
        parts[k] = w_in[:, :, o:o + n]
        o += n
    parts["dt"] = jnp.pad(parts["dt"], ((0, 0), (0, 0), (0, DT_PAD - 2 * SSD_HEADS)))
    return jnp.concatenate([parts[k] for k in SEG_SIZES], axis=-1).astype(BF16)


def _encoder_group(x, mod_tab, row0, weights, final_g):
    b, s, d = x.shape
    depth = weights["w_in"].shape[0]
    x2 = x.reshape(b * s, d)
    cos_tab, sin_tab = _rope_tables(s)
    for li in range(depth):
        w = {k: v[li] for k, v in weights.items()}
        z, xbc, dtr, gq, gk, gv, dq, dk, dv = in_projection(
            x2, mod_tab, li, row0, s, w["norm1_g"], w["w_in"], w["q_norm_g"], w["k_norm_g"],
            cos_tab, sin_tab)

        y_ssd = ssd_mixer(z, xbc, dtr, w["conv_w"], w["conv_b"], w["dt_bias"], w["a_log"], w["d_skip"],
                          w["ssd_norm_g"], b, s)
        y_gqa = gqa_attention(gq, gk, gv, b, s)
        lam_init = 0.8 - 0.6 * math.exp(-0.3 * li)
        y_diff = diff_attention(dq, dk, dv, w["diff_lambda"], w["diff_subln_g"], lam_init, b, s)

        x2 = out_projection(x2, y_ssd, y_gqa, y_diff, w["w_out"], mod_tab, li, row0, s)
        x2 = ffn(x2, mod_tab, li, row0, s, w["norm2_g"], w["w_gate"], w["w_up"], w["w_down"])
    return final_norm(x2, final_g).reshape(b, s, d)


def kernel(x_prompt, x_sample, c_prompt, c_sample, w_mod, b_mod, norm1_g, w_in, conv_w, conv_b, dt_bias,
           a_log, d_skip, ssd_norm_g, q_norm_g, k_norm_g, diff_lambda, diff_subln_g, w_out, norm2_g,
           w_gate, w_up, w_down, final_g):
    depth, d, _ = w_mod.shape
    bp, bs = c_prompt.shape[0], c_sample.shape[0]
    assert bp + bs <= MOD_ROWS
    c_all = jnp.concatenate([c_prompt, c_sample, jnp.zeros((MOD_ROWS - bp - bs, d), F32)], axis=0)
    mod = modulation(c_all, w_mod, b_mod)
    mod_tab = mod.reshape(depth * MOD_ROWS * 6, 1, d)

    weights = dict(
        norm1_g=norm1_g, w_in=_relayout_w_in(w_in), conv_w=conv_w, conv_b=conv_b, dt_bias=dt_bias,
        a_log=a_log, d_skip=d_skip, ssd_norm_g=ssd_norm_g, q_norm_g=q_norm_g, k_norm_g=k_norm_g,
        diff_lambda=diff_lambda, diff_subln_g=diff_subln_g, w_out=w_out.astype(BF16), norm2_g=norm2_g,
        w_gate=w_gate.astype(BF16), w_up=w_up.astype(BF16), w_down=w_down.astype(BF16))
    y_prompt = _encoder_group(x_prompt, mod_tab, 0, weights, final_g)
    y_sample = _encoder_group(x_sample, mod_tab, bp, weights, final_g)
    return (y_prompt, y_sample)
```
